```python
import math
import jax, jax.numpy as jnp
from jax import lax
import numpy as np

D_MODEL = 1024
BATCH = 32
SEQ = 2048
DEPTH = 2
DEC_BATCH = 32
DEC_SEQ = 16
PAST_LEN = 1024

CHUNK = 64
Q_BLOCK = 128
N_A_LAYERS = DEPTH // 2
N_B_LAYERS = DEPTH - N_A_LAYERS
GLA_HEADS = 4
GLA_DK = D_MODEL // 2
GLA_DV = D_MODEL
GLA_DKH = GLA_DK // GLA_HEADS
GLA_DVH = GLA_DV // GLA_HEADS
GATE_RANK = 16
GATE_TAU = 16.0
A_IN = 2 * GLA_DK + 2 * GLA_DV + GATE_RANK
DIFF_HEADS = 8
DIFF_HD = D_MODEL // (2 * DIFF_HEADS)
DIFF_QK = DIFF_HEADS * 2 * DIFF_HD
DIFF_V = DIFF_HEADS * 2 * DIFF_HD
ROPE_DIMS = DIFF_HD // 4
ROPE_THETA = 500000.0
D_FF = 4 * D_MODEL
DEEPNORM_ALPHA = (2 * DEPTH) ** 0.25
DEEPNORM_BETA = (8 * DEPTH) ** -0.25
LN_EPS = 1e-5
RMS_EPS = 1e-5

kernel_name = 'yoco_gla_diffattn_stream_step'


def layer_norm(x, g, b):
    xf = x.astype(jnp.float32)
    mu = jnp.mean(xf, -1, keepdims=True)
    var = jnp.mean(jnp.square(xf - mu), -1, keepdims=True)
    return ((xf - mu) * lax.rsqrt(var + LN_EPS) * g + b).astype(x.dtype)


def rms_norm(x, g):
    xf = x.astype(jnp.float32)
    return (xf * lax.rsqrt(jnp.mean(xf * xf, -1, keepdims=True) + RMS_EPS) * g).astype(x.dtype)


def partial_rope(x, pos):
    inv = 1.0 / (ROPE_THETA ** (jnp.arange(0, ROPE_DIMS, 2, dtype=jnp.float32) / ROPE_DIMS))
    ang = pos.astype(jnp.float32)[:, None] * inv[None, :]
    cos = jnp.cos(ang)[:, None, None, :]
    sin = jnp.sin(ang)[:, None, None, :]
    xr = x[..., :ROPE_DIMS].astype(jnp.float32)
    x1, x2 = xr[..., :ROPE_DIMS // 2], xr[..., ROPE_DIMS // 2:]
    rot = jnp.concatenate([x1 * cos - x2 * sin, x2 * cos + x1 * sin], -1)
    return jnp.concatenate([rot.astype(x.dtype), x[..., ROPE_DIMS:]], -1)


def gla_project(x, w_in, w_gate_up, b_gate):
    B, T, _ = x.shape
    h = x @ w_in
    q, k, v, r, gl = jnp.split(h, [GLA_DK, 2 * GLA_DK, 2 * GLA_DK + GLA_DV, 2 * GLA_DK + 2 * GLA_DV], axis=-1)
    logit = (gl @ w_gate_up + b_gate).astype(jnp.float32)
    log_a = (jax.nn.log_sigmoid(logit) / GATE_TAU).reshape(B, T, GLA_HEADS, GLA_DKH)
    q = q.reshape(B, T, GLA_HEADS, GLA_DKH) * (GLA_DKH ** -0.5)
    k = k.reshape(B, T, GLA_HEADS, GLA_DKH)
    v = v.reshape(B, T, GLA_HEADS, GLA_DVH)
    return q, k, v, log_a, r


def gla_block(S, q, k, v, log_a):
    C = q.shape[1]
    b = jnp.cumsum(log_a, axis=1)
    qf = q.astype(jnp.float32) * jnp.exp(b)
    kf = k.astype(jnp.float32) * jnp.exp(-b)
    vf = v.astype(jnp.float32)
    inter = jnp.einsum('bchk,bhkv->bchv', qf, S)
    att = jnp.einsum('bihk,bjhk->bhij', qf, kf)
    att = jnp.where(jnp.tril(jnp.ones((C, C), dtype=bool)), att, 0.0)
    intra = jnp.einsum('bhij,bjhv->bihv', att, vf)
    b_last = b[:, -1]
    k_tail = k.astype(jnp.float32) * jnp.exp(b_last[:, None] - b)
    S_new = S * jnp.exp(b_last)[..., None] + jnp.einsum('bjhk,bjhv->bhkv', k_tail, vf)
    return S_new, inter + intra


def gla_prompt(q, k, v, log_a):
    B, T = q.shape[:2]
    n = T // CHUNK

    def to_blocks(a):
        return a.reshape(B, n, CHUNK, *a.shape[2:]).swapaxes(0, 1)

    S0 = jnp.zeros((B, GLA_HEADS, GLA_DKH, GLA_DVH), jnp.float32)

    def step(S, inp):
        return gla_block(S, *inp)

    S_fin, o = lax.scan(step, S0, (to_blocks(q), to_blocks(k), to_blocks(v), to_blocks(log_a)))
    return o.swapaxes(0, 1).reshape(B, T, GLA_HEADS, GLA_DVH), S_fin


def gla_output(o, r, g_norm, w_o):
    B, T = o.shape[:2]
    o = rms_norm(o, g_norm).reshape(B, T, GLA_DV).astype(r.dtype)
    return (o * jax.nn.silu(r)) @ w_o


def shared_kv(h, w_kv, pos):
    B, T, _ = h.shape
    kv = h @ w_kv
    k, v = jnp.split(kv, [DIFF_QK], axis=-1)
    k = partial_rope(k.reshape(B, T, DIFF_HEADS, 2, DIFF_HD), pos)
    v = v.reshape(B, T, DIFF_HEADS, 2 * DIFF_HD)
    return k, v


def diff_query(x, w_q, pos):
    B, T, _ = x.shape
    q = (x @ w_q).reshape(B, T, DIFF_HEADS, 2, DIFF_HD)
    return partial_rope(q, pos) * (DIFF_HD ** -0.5)


def diff_lambda(lam, lambda_init):
    lam = lam.astype(jnp.float32)
    return jnp.exp(jnp.sum(lam[0] * lam[1])) - jnp.exp(jnp.sum(lam[2] * lam[3])) + lambda_init


def diff_attn_prompt(q, k, v, lam):
    T = q.shape[1]
    chunk_id = jnp.arange(T) // CHUNK
    outs = []
    for i in range(T // Q_BLOCK):
        qs, qe = i * Q_BLOCK, (i + 1) * Q_BLOCK
        s = jnp.einsum('bqhcd,bkhcd->bhcqk', q[:, qs:qe], k[:, :qe]).astype(jnp.float32)
        mask = chunk_id[None, :qe] <= chunk_id[qs:qe, None]
        p = jax.nn.softmax(jnp.where(mask, s, -jnp.inf), axis=-1)
        a = (p[:, :, 0] - lam * p[:, :, 1]).astype(v.dtype)
        outs.append(jnp.einsum('bhqk,bkhv->bqhv', a, v[:, :qe]))
    return jnp.concatenate(outs, axis=1)


def diff_attn_sample(q, k_new, v_new, cache_k, cache_v, lam):
    P = cache_k.shape[1]
    s_past = jnp.einsum('bqhcd,bkhcd->bhcqk', q, cache_k)
    s_new = jnp.einsum('bqhcd,bkhcd->bhcqk', q, k_new)
    s = jnp.concatenate([s_past, s_new], axis=-1).astype(jnp.float32)
    p = jax.nn.softmax(s, axis=-1)
    a = (p[:, :, 0] - lam * p[:, :, 1]).astype(v_new.dtype)
    return (jnp.einsum('bhqk,bkhv->bqhv', a[..., :P], cache_v)
            + jnp.einsum('bhqk,bkhv->bqhv', a[..., P:], v_new))


def diff_output(o, g_norm, lambda_init, w_o):
    B, T = o.shape[:2]
    o = rms_norm(o, g_norm) * (1.0 - lambda_init)
    return o.reshape(B, T, DIFF_V) @ w_o


def sqrelu_mlp(x, w_up, w_down):
    return jnp.square(jax.nn.relu(x @ w_up)) @ w_down


def run_trunk(x, pos, state_gla, cache_k, cache_v, w_in_a, w_gate_up_a, b_gate_a, g_norm_a, w_o_a,
              w_kv, w_q_b, lam_b, g_norm_b, w_o_b, w_up, w_down, ln_g, ln_b):
    gla_states = []
    k_sh = v_sh = None
    for layer in range(DEPTH):
        if layer < N_A_LAYERS:
            i = layer
            q, k, v, log_a, r = gla_project(x, w_in_a[i], w_gate_up_a[i], b_gate_a[i])
            if state_gla is None:
                o, s_fin = gla_prompt(q, k, v, log_a)
            else:
                s_fin, o = gla_block(state_gla[i].astype(jnp.float32), q, k, v, log_a)
            gla_states.append(s_fin.astype(x.dtype))
            mix = gla_output(o, r, g_norm_a[i], w_o_a[i])
        else:
            j = layer - N_A_LAYERS
            lambda_init = 0.8 - 0.6 * math.exp(-0.3 * layer)
            lam = diff_lambda(lam_b[j], lambda_init)
            q = diff_query(x, w_q_b[j], pos)
            if cache_k is None:
                o = diff_attn_prompt(q, k_sh, v_sh, lam)
            else:
                o = diff_attn_sample(q, k_sh, v_sh, cache_k, cache_v, lam)
            mix = diff_output(o, g_norm_b[j], lambda_init, w_o_b[j])
        x = layer_norm(DEEPNORM_ALPHA * x + mix.astype(x.dtype), ln_g[layer, 0], ln_b[layer, 0])
        x = layer_norm(DEEPNORM_ALPHA * x + sqrelu_mlp(x, w_up[layer], w_down[layer]).astype(x.dtype),
                       ln_g[layer, 1], ln_b[layer, 1])
        if layer == N_A_LAYERS - 1:
            k_sh, v_sh = shared_kv(x, w_kv, pos)
    return x, jnp.stack(gla_states), k_sh, v_sh


def setup_inputs(seed: int = 0) -> dict:
    key = jax.random.key(seed)
    ks = jax.random.split(key, 19)

    def nrm(k, shape, scale):
        return jax.random.normal(k, shape, jnp.float32) * scale

    return {
        'x_prompt': nrm(ks[0], (BATCH, SEQ, D_MODEL), 1.0),
        'x_sample': nrm(ks[1], (DEC_BATCH, DEC_SEQ, D_MODEL), 1.0),
        'state_gla': nrm(ks[2], (N_A_LAYERS, DEC_BATCH, GLA_HEADS, GLA_DKH, GLA_DVH), 0.5),
        'cache_k': nrm(ks[3], (DEC_BATCH, PAST_LEN, DIFF_HEADS, 2, DIFF_HD), 1.0),
        'cache_v': nrm(ks[4], (DEC_BATCH, PAST_LEN, DIFF_HEADS, 2 * DIFF_HD), 1.0),
        'w_in_a': nrm(ks[5], (N_A_LAYERS, D_MODEL, A_IN), D_MODEL ** -0.5),
        'w_gate_up_a': nrm(ks[6], (N_A_LAYERS, GATE_RANK, GLA_DK), GATE_RANK ** -0.5),
        'b_gate_a': nrm(ks[7], (N_A_LAYERS, GLA_DK), 0.1),
        'g_norm_a': 1.0 + nrm(ks[8], (N_A_LAYERS, GLA_DVH), 0.01),
        'w_o_a': nrm(ks[9], (N_A_LAYERS, GLA_DV, D_MODEL), GLA_DV ** -0.5 * DEEPNORM_BETA),
        'w_kv': nrm(ks[10], (D_MODEL, DIFF_QK + DIFF_V), D_MODEL ** -0.5),
        'w_q_b': nrm(ks[11], (N_B_LAYERS, D_MODEL, DIFF_QK), D_MODEL ** -0.5),
        'lam_b': nrm(ks[12], (N_B_LAYERS, 4, DIFF_HD), 0.1),
        'g_norm_b': 1.0 + nrm(ks[13], (N_B_LAYERS, 2 * DIFF_HD), 0.01),
        'w_o_b': nrm(ks[14], (N_B_LAYERS, DIFF_V, D_MODEL), DIFF_V ** -0.5 * DEEPNORM_BETA),
        'w_up': nrm(ks[15], (DEPTH, D_MODEL, D_FF), D_MODEL ** -0.5),
        'w_down': nrm(ks[16], (DEPTH, D_FF, D_MODEL), D_FF ** -0.5 * DEEPNORM_BETA),
        'ln_g': 1.0 + nrm(ks[17], (DEPTH, 2, D_MODEL), 0.01),
        'ln_b': nrm(ks[18], (DEPTH, 2, D_MODEL), 0.01),
    }


def reference(x_prompt, x_sample, state_gla, cache_k, cache_v, w_in_a, w_gate_up_a, b_gate_a, g_norm_a,
              w_o_a, w_kv, w_q_b, lam_b, g_norm_b, w_o_b, w_up, w_down, ln_g, ln_b):
    pos_prompt = jnp.arange(x_prompt.shape[1])
    pos_sample = cache_k.shape[1] + jnp.arange(x_sample.shape[1])
    y_prompt, gla_prompt_state, k_prompt, v_prompt = run_trunk(
        x_prompt, pos_prompt, None, None, None, w_in_a, w_gate_up_a, b_gate_a, g_norm_a, w_o_a,
        w_kv, w_q_b, lam_b, g_norm_b, w_o_b, w_up, w_down, ln_g, ln_b)
    y_sample, gla_sample_state, k_sample, v_sample = run_trunk(
        x_sample, pos_sample, state_gla, cache_k, cache_v, w_in_a, w_gate_up_a, b_gate_a, g_norm_a, w_o_a,
        w_kv, w_q_b, lam_b, g_norm_b, w_o_b, w_up, w_down, ln_g, ln_b)
    return (y_prompt, y_sample, gla_prompt_state, k_prompt, v_prompt, gla_sample_state, k_sample, v_sample)
```

```python
import functools
import math

import jax
import jax.numpy as jnp
from jax import lax
from jax.experimental import pallas as pl
from jax.experimental.pallas import tpu as pltpu

F32 = jnp.float32
BF16 = jnp.bfloat16

D_MODEL = 1024
DEPTH = 2
CHUNK = 64
GLA_HEADS = 4
GLA_DK = D_MODEL // 2
GLA_DV = D_MODEL
GLA_DKH = GLA_DK // GLA_HEADS
GLA_DVH = GLA_DV // GLA_HEADS
GATE_RANK = 16
GATE_TAU = 16.0
DIFF_HEADS = 8
DIFF_HD = D_MODEL // (2 * DIFF_HEADS)
HEAD_W = 2 * DIFF_HD
ROPE_DIMS = DIFF_HD // 4
ROPE_HALF = ROPE_DIMS // 2
ROPE_THETA = 500000.0
D_FF = 4 * D_MODEL
DEEPNORM_ALPHA = (2 * DEPTH) ** 0.25
LN_EPS = 1e-5
RMS_EPS = 1e-5
LAMBDA_INIT = 0.8 - 0.6 * math.exp(-0.3 * 1)

LANES = 128
VMEM_LIMIT = 56 * 1024 * 1024
TOKEN_TILE = 512
GLA_TILE = 256
ATT_TILE = 256
FF_CHUNK = 1024


def _dot(a, b):
    return jnp.dot(a, b, preferred_element_type=F32)


def _dot_nt(a, b):
    return lax.dot_general(a, b, (((1,), (1,)), ((), ())), preferred_element_type=F32)


def _layer_norm(z, g, b):
    mu = jnp.mean(z, axis=-1, keepdims=True)
    zc = z - mu
    var = jnp.mean(zc * zc, axis=-1, keepdims=True)
    return zc * lax.rsqrt(var + LN_EPS) * g + b


def _rms_scale(x):
    return lax.rsqrt(jnp.mean(x * x, axis=-1, keepdims=True) + RMS_EPS)


def _resident(shape):
    zeros = (0,) * len(shape)
    return pl.BlockSpec(shape, lambda *_: zeros, pipeline_mode=pl.Buffered(1))


def _params(semantics):
    return pltpu.CompilerParams(dimension_semantics=semantics, vmem_limit_bytes=VMEM_LIMIT)


def _inproj_kernel(x_ref, wm_ref, wgl_ref, wg_ref, bg_ref, q_ref, k_ref, v_ref, r_ref, la_ref):
    xb = x_ref[...].astype(BF16)
    q_ref[...] = _dot(xb, wm_ref[:, 0:GLA_DK]) * (GLA_DKH ** -0.5)
    k_ref[...] = _dot(xb, wm_ref[:, GLA_DK:2 * GLA_DK])
    v_ref[...] = _dot(xb, wm_ref[:, 2 * GLA_DK:2 * GLA_DK + GLA_DV]).astype(BF16)
    r_ref[...] = _dot(xb, wm_ref[:, 2 * GLA_DK + GLA_DV:])
    gl = _dot(xb, wgl_ref[...]).astype(BF16)
    logit = _dot(gl, wg_ref[...]) + bg_ref[...]
    log_sig = jnp.minimum(logit, 0.0) - jnp.log1p(jnp.exp(-jnp.abs(logit)))
    la_ref[...] = log_sig * (1.0 / GATE_TAU)


def _gla_inproj(x2d, wm, wgl, wg, bg):
    n = x2d.shape[0]
    tm = min(TOKEN_TILE, n)
    row = lambda w: pl.BlockSpec((tm, w), lambda i: (i, 0))
    return pl.pallas_call(
        _inproj_kernel,
        grid=(n // tm,),
        in_specs=[row(D_MODEL), _resident(wm.shape), _resident(wgl.shape), _resident(wg.shape),
                  _resident(bg.shape)],
        out_specs=[row(GLA_DK), row(GLA_DK), row(GLA_DV), row(GLA_DV), row(GLA_DK)],
        out_shape=[jax.ShapeDtypeStruct((n, GLA_DK), F32), jax.ShapeDtypeStruct((n, GLA_DK), F32),
                   jax.ShapeDtypeStruct((n, GLA_DV), BF16), jax.ShapeDtypeStruct((n, GLA_DV), F32),
                   jax.ShapeDtypeStruct((n, GLA_DK), F32)],
        compiler_params=_params(("parallel",)),
        name="gla_inproj",
    )(x2d, wm, wgl, wg, bg)


def _gla_kernel(q_ref, k_ref, la_ref, v_ref, r_ref, x_ref, si_ref, gn_ref, wo_ref, lng_ref, lnb_ref,
                x1_ref, so_ref, o_scr, *, tile, chunk):
    @pl.when(pl.program_id(1) == 0)
    def _():
        so_ref[...] = si_ref[...]

    row = lax.broadcasted_iota(jnp.int32, (chunk, GLA_DK), 0)
    ci = lax.broadcasted_iota(jnp.int32, (chunk, chunk), 0)
    cj = lax.broadcasted_iota(jnp.int32, (chunk, chunk), 1)
    causal = cj <= ci

    def chunk_step(c, carry):
        r0 = pl.multiple_of(c * chunk, chunk)
        q = q_ref[pl.ds(r0, chunk), :]
        k = k_ref[pl.ds(r0, chunk), :]
        v = v_ref[pl.ds(r0, chunk), :]
        b = la_ref[pl.ds(r0, chunk), :]
        s = 1
        while s < chunk:
            b = b + jnp.where(row >= s, pltpu.roll(b, s, axis=0), 0.0)
            s *= 2
        qf = q * jnp.exp(b)
        kf = k * jnp.exp(-b)
        b_last = b[chunk - 1:chunk, :]
        k_tail = k * jnp.exp(b_last - b)
        for h in range(GLA_HEADS):
            ks = slice(h * GLA_DKH, (h + 1) * GLA_DKH)
            vs = slice(h * GLA_DVH, (h + 1) * GLA_DVH)
            qh = qf[:, ks].astype(BF16)
            kh = kf[:, ks].astype(BF16)
            vh = v[:, vs]
            state = so_ref[0, h]
            inter = _dot(qh, state.astype(BF16))
            att = jnp.where(causal, _dot_nt(qh, kh), 0.0).astype(BF16)
            o_scr[pl.ds(r0, chunk), vs] = inter + _dot(att, vh)
            decay = jnp.exp(b[:, ks].T[:, chunk - 1:chunk])
            so_ref[0, h] = state * decay + _dot(k_tail[:, ks].T.astype(BF16), vh)
        return carry

    lax.fori_loop(0, tile // chunk, chunk_step, 0)

    o = o_scr[...]
    parts = []
    for h in range(GLA_HEADS):
        oh = o[:, h * GLA_DVH:(h + 1) * GLA_DVH]
        parts.append(oh * _rms_scale(oh) * gn_ref[...])
    r = r_ref[...]
    gated = (jnp.concatenate(parts, axis=-1) * (r * (1.0 / (1.0 + jnp.exp(-r))))).astype(BF16)
    mix = _dot(gated, wo_ref[...])
    x1_ref[...] = _layer_norm(DEEPNORM_ALPHA * x_ref[...] + mix, lng_ref[...], lnb_ref[...])


def _gla_mix(q, k, la, v, r, x2d, state_in, gn, wo, lng, lnb, *, batch, seq, chunk):
    n = x2d.shape[0]
    tile = min(GLA_TILE, seq)
    nt = seq // tile
    row = lambda w: pl.BlockSpec((tile, w), lambda b, t: (b * nt + t, 0))
    st = pl.BlockSpec((1, GLA_HEADS, GLA_DKH, GLA_DVH), lambda b, t: (b, 0, 0, 0))
    return pl.pallas_call(
        functools.partial(_gla_kernel, tile=tile, chunk=chunk),
        grid=(batch, nt),
        in_specs=[row(GLA_DK), row(GLA_DK), row(GLA_DK), row(GLA_DV), row(GLA_DV), row(D_MODEL), st,
                  _resident(gn.shape), _resident(wo.shape), _resident(lng.shape), _resident(lnb.shape)],
        out_specs=[row(D_MODEL), st],
        out_shape=[jax.ShapeDtypeStruct((n, D_MODEL), F32),
                   jax.ShapeDtypeStruct((batch, GLA_HEADS, GLA_DKH, GLA_DVH), F32)],
        scratch_shapes=[pltpu.VMEM((tile, GLA_DV), F32)],
        compiler_params=_params(("parallel", "arbitrary")),
        name="gla_mix",
    )(q, k, la, v, r, x2d, state_in, gn, wo, lng, lnb)


def _mlp(xb, wup_ref, wdn_ref):
    acc = None
    for c in range(D_FF // FF_CHUNK):
        cs = slice(c * FF_CHUNK, (c + 1) * FF_CHUNK)
        hid = jnp.maximum(_dot(xb, wup_ref[:, cs]), 0.0)
        part = _dot((hid * hid).astype(BF16), wdn_ref[cs, :])
        acc = part if acc is None else acc + part
    return acc


def _rope(x, cos, sin):
    lane = lax.broadcasted_iota(jnp.int32, (x.shape[0], LANES), 1) & (DIFF_HD - 1)
    first = lane < ROPE_HALF
    second = (lane >= ROPE_HALF) & (lane < ROPE_DIMS)
    parts = []
    for j in range(x.shape[1] // LANES):
        xj = x[:, j * LANES:(j + 1) * LANES]
        ahead = pltpu.roll(xj, LANES - ROPE_HALF, axis=1)
        behind = pltpu.roll(xj, ROPE_HALF, axis=1)
        parts.append(jnp.where(first, xj * cos - ahead * sin,
                               jnp.where(second, xj * cos + behind * sin, xj)))
    return jnp.concatenate(parts, axis=-1)


def _mlp_kv_kernel(x1_ref, wup_ref, wdn_ref, lng_ref, lnb_ref, wkv_ref, wq_ref, cos_ref, sin_ref,
                   x2_ref, k_ref, v_ref, q_ref):
    x1 = x1_ref[...]
    y = _mlp(x1.astype(BF16), wup_ref, wdn_ref)
    x2 = _layer_norm(DEEPNORM_ALPHA * x1 + y, lng_ref[...], lnb_ref[...])
    x2_ref[...] = x2
    x2b = x2.astype(BF16)
    cos = cos_ref[...]
    sin = sin_ref[...]
    k_ref[...] = _rope(_dot(x2b, wkv_ref[:, 0:D_MODEL]), cos, sin)
    v_ref[...] = _dot(x2b, wkv_ref[:, D_MODEL:])
    q_ref[...] = (_rope(_dot(x2b, wq_ref[...]), cos, sin) * (DIFF_HD ** -0.5)).astype(BF16)


def _mlp_kv(x1, wup, wdn, lng, lnb, wkv, wq, cos, sin, *, seq):
    n = x1.shape[0]
    tm = min(TOKEN_TILE, n)
    if tm > seq:
        cos = jnp.tile(cos, (tm // seq, 1))
        sin = jnp.tile(sin, (tm // seq, 1))
    npos = max(seq // tm, 1)
    row = lambda w: pl.BlockSpec((tm, w), lambda i: (i, 0))
    pos = pl.BlockSpec((tm, LANES), lambda i: (i % npos, 0))
    return pl.pallas_call(
        _mlp_kv_kernel,
        grid=(n // tm,),
        in_specs=[row(D_MODEL), _resident(wup.shape), _resident(wdn.shape), _resident(lng.shape),
                  _resident(lnb.shape), _resident(wkv.shape), _resident(wq.shape), pos, pos],
        out_specs=[row(D_MODEL), row(D_MODEL), row(D_MODEL), row(D_MODEL)],
        out_shape=[jax.ShapeDtypeStruct((n, D_MODEL), F32), jax.ShapeDtypeStruct((n, D_MODEL), F32),
                   jax.ShapeDtypeStruct((n, D_MODEL), F32), jax.ShapeDtypeStruct((n, D_MODEL), BF16)],
        compiler_params=_params(("parallel",)),
        name="mlp_kv",
    )(x1, wup, wdn, lng, lnb, wkv, wq, cos, sin)


def _out_mlp_kernel(o_ref, x2_ref, wo_ref, lng1_ref, lnb1_ref, wup_ref, wdn_ref, lng2_ref, lnb2_ref, y_ref):
    mix = _dot(o_ref[...], wo_ref[...])
    x3 = _layer_norm(DEEPNORM_ALPHA * x2_ref[...] + mix, lng1_ref[...], lnb1_ref[...])
    y = _mlp(x3.astype(BF16), wup_ref, wdn_ref)
    y_ref[...] = _layer_norm(DEEPNORM_ALPHA * x3 + y, lng2_ref[...], lnb2_ref[...])


def _out_mlp(o, x2, wo, lng1, lnb1, wup, wdn, lng2, lnb2):
    n = x2.shape[0]
    tm = min(TOKEN_TILE, n)
    row = lambda w: pl.BlockSpec((tm, w), lambda i: (i, 0))
    return pl.pallas_call(
        _out_mlp_kernel,
        grid=(n // tm,),
        in_specs=[row(D_MODEL), row(D_MODEL), _resident(wo.shape), _resident(lng1.shape),
                  _resident(lnb1.shape), _resident(wup.shape), _resident(wdn.shape),
                  _resident(lng2.shape), _resident(lnb2.shape)],
        out_specs=row(D_MODEL),
        out_shape=jax.ShapeDtypeStruct((n, D_MODEL), F32),
        compiler_params=_params(("parallel",)),
        name="out_mlp",
    )(o, x2, wo, lng1, lnb1, wup, wdn, lng2, lnb2)


def _lambda_full(lam):
    t1 = jnp.sum(lam[0:1, :] * lam[1:2, :], axis=-1, keepdims=True)
    t2 = jnp.sum(lam[2:3, :] * lam[3:4, :], axis=-1, keepdims=True)
    return jnp.exp(t1) - jnp.exp(t2) + LAMBDA_INIT


def _stack_maps(q):
    qf = q.astype(F32)
    lane = lax.broadcasted_iota(jnp.int32, qf.shape, 1)
    return jnp.concatenate([jnp.where(lane < DIFF_HD, qf, 0.0), jnp.where(lane >= DIFF_HD, qf, 0.0)],
                           axis=0).astype(BF16)


def _head_norm(acc, gn):
    return (acc * _rms_scale(acc) * gn * (1.0 - LAMBDA_INIT)).astype(BF16)


def _lane_fold(x, op):
    out = x[:, 0:LANES]
    for j in range(1, x.shape[1] // LANES):
        out = op(out, x[:, j * LANES:(j + 1) * LANES])
    return out


def _attn_kernel(lam_ref, gn_ref, q_ref, k_ref, v_ref, o_ref, kb, vb, s_scr, m_scr, l_scr, *, tq):
    i = pl.program_id(2)

    @pl.when(i == 0)
    def _():
        kb[...] = k_ref[0].astype(BF16)
        vb[...] = v_ref[0].astype(BF16)

    lam = _lambda_full(lam_ref[...])
    qq = _stack_maps(q_ref[0])

    def scores(j):
        return _dot_nt(qq, kb[pl.ds(pl.multiple_of(j * tq, tq), tq), :])

    m_scr[...] = jnp.full(m_scr.shape, -jnp.inf, F32)

    def pass_scores(j, carry):
        s = scores(j)
        s_scr[j] = s
        m_scr[...] = jnp.maximum(m_scr[...], _lane_fold(s, jnp.maximum))
        return carry

    lax.fori_loop(0, i, pass_scores, 0)
    chunk_shift = CHUNK.bit_length() - 1
    rc = (lax.broadcasted_iota(jnp.int32, (2 * tq, tq), 0) & (tq - 1)) >> chunk_shift
    cc = lax.broadcasted_iota(jnp.int32, (2 * tq, tq), 1) >> chunk_shift
    s = jnp.where(cc <= rc, scores(i), -jnp.inf)
    s_scr[i] = s
    m = jnp.max(jnp.maximum(m_scr[...], _lane_fold(s, jnp.maximum)), axis=-1, keepdims=True)

    l_scr[...] = jnp.zeros(l_scr.shape, F32)

    def pass_exp(j, carry):
        e = jnp.exp(s_scr[j] - m)
        s_scr[j] = e
        l_scr[...] += _lane_fold(e, jnp.add)
        return carry

    lax.fori_loop(0, i + 1, pass_exp, 0)
    l = jnp.sum(l_scr[...], axis=-1, keepdims=True)
    w0 = 1.0 / l[0:tq]
    w1 = lam / l[tq:]

    def pass_pv(j, acc):
        e = s_scr[j]
        a = (e[0:tq] * w0 - e[tq:] * w1).astype(BF16)
        return acc + _dot(a, vb[pl.ds(pl.multiple_of(j * tq, tq), tq), :])

    acc = lax.fori_loop(0, i + 1, pass_pv, jnp.zeros((tq, HEAD_W), F32))
    o_ref[0] = _head_norm(acc, gn_ref[...])


def _diff_attn_prompt(q, k, v, lam, gn):
    batch, seq, _ = q.shape
    tq = min(ATT_TILE, seq)
    nq = seq // tq
    qo = pl.BlockSpec((1, tq, HEAD_W), lambda b, h, i: (b, i, h))
    kv = pl.BlockSpec((1, seq, HEAD_W), lambda b, h, i: (b, 0, h))
    return pl.pallas_call(
        functools.partial(_attn_kernel, tq=tq),
        grid=(batch, DIFF_HEADS, nq),
        in_specs=[_resident(lam.shape), _resident(gn.shape), qo, kv, kv],
        out_specs=qo,
        out_shape=jax.ShapeDtypeStruct((batch, seq, D_MODEL), BF16),
        scratch_shapes=[pltpu.VMEM((seq, HEAD_W), BF16), pltpu.VMEM((seq, HEAD_W), BF16),
                        pltpu.VMEM((nq, 2 * tq, tq), F32), pltpu.VMEM((2 * tq, LANES), F32),
                        pltpu.VMEM((2 * tq, LANES), F32)],
        compiler_params=_params(("parallel", "parallel", "arbitrary")),
        name="diff_attn_prompt",
    )(lam, gn, q, k, v)


def _attn_sample_kernel(lam_ref, gn_ref, q_ref, ck_ref, cv_ref, kn_ref, vn_ref, o_ref):
    lam = _lambda_full(lam_ref[...])
    qq = _stack_maps(q_ref[0])
    t = q_ref.shape[1]
    sp = _dot_nt(qq, ck_ref[0].astype(BF16))
    sn = _dot_nt(qq, kn_ref[0].astype(BF16))
    m = jnp.maximum(jnp.max(sp, axis=-1, keepdims=True), jnp.max(sn, axis=-1, keepdims=True))
    ep = jnp.exp(sp - m)
    en = jnp.exp(sn - m)
    l = jnp.sum(ep, axis=-1, keepdims=True) + jnp.sum(en, axis=-1, keepdims=True)
    w0 = 1.0 / l[0:t]
    w1 = lam / l[t:]
    ap = (ep[0:t] * w0 - ep[t:] * w1).astype(BF16)
    an = (en[0:t] * w0 - en[t:] * w1).astype(BF16)
    acc = _dot(ap, cv_ref[0].astype(BF16)) + _dot(an, vn_ref[0].astype(BF16))
    o_ref[0] = _head_norm(acc, gn_ref[...])


def _diff_attn_sample(q, k_new, v_new, cache_k, cache_v, lam, gn):
    batch, t, _ = q.shape
    past = cache_k.shape[1]
    new = pl.BlockSpec((1, t, HEAD_W), lambda b, h: (b, 0, h))
    old = pl.BlockSpec((1, past, HEAD_W), lambda b, h: (b, 0, h))
    return pl.pallas_call(
        _attn_sample_kernel,
        grid=(batch, DIFF_HEADS),
        in_specs=[_resident(lam.shape), _resident(gn.shape), new, old, old, new, new],
        out_specs=new,
        out_shape=jax.ShapeDtypeStruct((batch, t, D_MODEL), BF16),
        compiler_params=_params(("parallel", "parallel")),
        name="diff_attn_sample",
    )(lam, gn, q, cache_k, cache_v, k_new, v_new)


def _rope_tables(pos):
    inv = 1.0 / (ROPE_THETA ** (jnp.arange(0, ROPE_DIMS, 2, dtype=F32) / ROPE_DIMS))
    ang = pos.astype(F32)[:, None] * inv[None, :]
    idx = jnp.arange(LANES) % ROPE_HALF
    return jnp.cos(ang)[:, idx], jnp.sin(ang)[:, idx]


def _prepare_weights(w_in_a, w_gate_up_a, b_gate_a, g_norm_a, w_o_a, w_kv, w_q_b, lam_b, g_norm_b, w_o_b,
                     w_up, w_down, ln_g, ln_b):
    n_main = 2 * GLA_DK + 2 * GLA_DV
    w_in = w_in_a[0]
    row = lambda a: a.reshape(1, -1)
    return dict(
        wm=w_in[:, :n_main].astype(BF16),
        wgl=jnp.pad(w_in[:, n_main:], ((0, 0), (0, LANES - GATE_RANK))).astype(BF16),
        wg=jnp.pad(w_gate_up_a[0], ((0, LANES - GATE_RANK), (0, 0))).astype(BF16),
        bg=row(b_gate_a[0]),
        gn_a=row(g_norm_a[0]),
        wo_a=w_o_a[0].astype(BF16),
        wkv=w_kv.astype(BF16),
        wq=w_q_b[0].astype(BF16),
        lam=lam_b[0],
        gn_b=row(g_norm_b[0]),
        wo_b=w_o_b[0].astype(BF16),
        wup=[w_up[i].astype(BF16) for i in range(DEPTH)],
        wdn=[w_down[i].astype(BF16) for i in range(DEPTH)],
        lng=[[row(ln_g[i, j]) for j in range(2)] for i in range(DEPTH)],
        lnb=[[row(ln_b[i, j]) for j in range(2)] for i in range(DEPTH)],
    )


def _trunk(x, pos, state_in, cache_k, cache_v, w):
    batch, seq, _ = x.shape
    n = batch * seq
    x2d = x.reshape(n, D_MODEL)
    chunk = min(CHUNK, seq)
    q, k, v, r, la = _gla_inproj(x2d, w["wm"], w["wgl"], w["wg"], w["bg"])
    x1, state = _gla_mix(q, k, la, v, r, x2d, state_in, w["gn_a"], w["wo_a"], w["lng"][0][0], w["lnb"][0][0],
                         batch=batch, seq=seq, chunk=chunk)
    cos, sin = _rope_tables(pos)
    x2, k_sh, v_sh, q_b = _mlp_kv(x1, w["wup"][0], w["wdn"][0], w["lng"][0][1], w["lnb"][0][1],
                                  w["wkv"], w["wq"], cos, sin, seq=seq)
    k3 = k_sh.reshape(batch, seq, D_MODEL)
    v3 = v_sh.reshape(batch, seq, D_MODEL)
    q3 = q_b.reshape(batch, seq, D_MODEL)
    if cache_k is None:
        o = _diff_attn_prompt(q3, k3, v3, w["lam"], w["gn_b"])
    else:
        past = cache_k.shape[1]
        o = _diff_attn_sample(q3, k3, v3, cache_k.reshape(batch, past, D_MODEL),
                              cache_v.reshape(batch, past, D_MODEL), w["lam"], w["gn_b"])
    y = _out_mlp(o.reshape(n, D_MODEL), x2, w["wo_b"], w["lng"][1][0], w["lnb"][1][0],
                 w["wup"][1], w["wdn"][1], w["lng"][1][1], w["lnb"][1][1])
    return (y.reshape(batch, seq, D_MODEL), state[None],
            k3.reshape(batch, seq, DIFF_HEADS, 2, DIFF_HD), v3.reshape(batch, seq, DIFF_HEADS, HEAD_W))


def kernel(x_prompt, x_sample, state_gla, cache_k, cache_v, w_in_a, w_gate_up_a, b_gate_a, g_norm_a, w_o_a, w_kv, w_q_b, lam_b, g_norm_b, w_o_b, w_up, w_down, ln_g, ln_b):
    w = _prepare_weights(w_in_a, w_gate_up_a, b_gate_a, g_norm_a, w_o_a, w_kv, w_q_b, lam_b, g_norm_b,
                         w_o_b, w_up, w_down, ln_g, ln_b)
    batch, seq, _ = x_prompt.shape
    past = cache_k.shape[1]
    zero_state = jnp.zeros((batch, GLA_HEADS, GLA_DKH, GLA_DVH), F32)
    y_p, s_p, k_p, v_p = _trunk(x_prompt, jnp.arange(seq), zero_state, None, None, w)
    y_s, s_s, k_s, v_s = _trunk(x_sample, past + jnp.arange(x_sample.shape[1]), state_gla[0],
                                cache_k, cache_v, w)
    return (y_p, y_s, s_p, k_p, v_p, s_s, k_s, v_s)
```

```python
import functools
import math

import jax
import jax.numpy as jnp
from jax import lax
from jax.experimental import pallas as pl
from jax.experimental.pallas import tpu as pltpu

F32 = jnp.float32
BF16 = jnp.bfloat16

D_MODEL = 1024
DEPTH = 2
CHUNK = 64
GLA_HEADS = 4
GLA_DK = D_MODEL // 2
GLA_DV = D_MODEL
GLA_DKH = GLA_DK // GLA_HEADS
GLA_DVH = GLA_DV // GLA_HEADS
GATE_RANK = 16
GATE_TAU = 16.0
DIFF_HEADS = 8
DIFF_HD = D_MODEL // (2 * DIFF_HEADS)
HEAD_W = 2 * DIFF_HD
ROPE_DIMS = DIFF_HD // 4
ROPE_HALF = ROPE_DIMS // 2
ROPE_THETA = 500000.0
D_FF = 4 * D_MODEL
DEEPNORM_ALPHA = (2 * DEPTH) ** 0.25
LN_EPS = 1e-5
RMS_EPS = 1e-5
LAMBDA_INIT = 0.8 - 0.6 * math.exp(-0.3 * 1)
LOG2E = math.log2(math.e)

LANES = 128
SUBLANES = 8
VMEM_LIMIT = 56 * 1024 * 1024
TOKEN_TILE = 512
GLA_TILE = 256
ATT_TILE = 256
FF_CHUNK = 1024

assert ROPE_HALF == SUBLANES


def _dot(a, b):
    return jnp.dot(a, b, preferred_element_type=F32)


def _dot_nt(a, b):
    return lax.dot_general(a, b, (((1,), (1,)), ((), ())), preferred_element_type=F32)


def _layer_norm(z, g, b):
    mu = jnp.mean(z, axis=-1, keepdims=True)
    zc = z - mu
    var = jnp.mean(zc * zc, axis=-1, keepdims=True)
    return zc * lax.rsqrt(var + LN_EPS) * g + b


def _rms_scale(x):
    return lax.rsqrt(jnp.mean(x * x, axis=-1, keepdims=True) + RMS_EPS)


def _resident(shape):
    zeros = (0,) * len(shape)
    return pl.BlockSpec(shape, lambda *_: zeros, pipeline_mode=pl.Buffered(1))


def _params(semantics):
    return pltpu.CompilerParams(dimension_semantics=semantics, vmem_limit_bytes=VMEM_LIMIT)


def _inproj_kernel(x_ref, wm_ref, wgl_ref, wg_ref, bg_ref, q_ref, k_ref, v_ref, r_ref, la_ref):
    xb = x_ref[...].astype(BF16)
    q_ref[...] = _dot(xb, wm_ref[:, 0:GLA_DK]) * (GLA_DKH ** -0.5)
    k_ref[...] = _dot(xb, wm_ref[:, GLA_DK:2 * GLA_DK])
    v_ref[...] = _dot(xb, wm_ref[:, 2 * GLA_DK:2 * GLA_DK + GLA_DV]).astype(BF16)
    r_ref[...] = _dot(xb, wm_ref[:, 2 * GLA_DK + GLA_DV:])
    gl = _dot(xb, wgl_ref[...]).astype(BF16)
    logit = _dot(gl, wg_ref[...]) + bg_ref[...]
    log_sig = jnp.minimum(logit, 0.0) - jnp.log1p(jnp.exp(-jnp.abs(logit)))
    la_ref[...] = log_sig * (1.0 / GATE_TAU)


def _gla_inproj(x2d, wm, wgl, wg, bg):
    n = x2d.shape[0]
    tm = min(TOKEN_TILE, n)
    row = lambda w: pl.BlockSpec((tm, w), lambda i: (i, 0))
    return pl.pallas_call(
        _inproj_kernel,
        grid=(n // tm,),
        in_specs=[row(D_MODEL), _resident(wm.shape), _resident(wgl.shape), _resident(wg.shape),
                  _resident(bg.shape)],
        out_specs=[row(GLA_DK), row(GLA_DK), row(GLA_DV), row(GLA_DV), row(GLA_DK)],
        out_shape=[jax.ShapeDtypeStruct((n, GLA_DK), F32), jax.ShapeDtypeStruct((n, GLA_DK), F32),
                   jax.ShapeDtypeStruct((n, GLA_DV), BF16), jax.ShapeDtypeStruct((n, GLA_DV), F32),
                   jax.ShapeDtypeStruct((n, GLA_DK), F32)],
        compiler_params=_params(("parallel",)),
        name="gla_inproj",
    )(x2d, wm, wgl, wg, bg)


def _gla_kernel(q_ref, k_ref, la_ref, v_ref, r_ref, x_ref, si_ref, gn_ref, wo_ref, lng_ref, lnb_ref,
                x1_ref, so_ref, o_scr, *, tile, chunk):
    @pl.when(pl.program_id(1) == 0)
    def _():
        so_ref[...] = si_ref[...]

    row = lax.broadcasted_iota(jnp.int32, (chunk, GLA_DK), 0)
    ci = lax.broadcasted_iota(jnp.int32, (chunk, chunk), 0)
    cj = lax.broadcasted_iota(jnp.int32, (chunk, chunk), 1)
    causal = cj <= ci

    def chunk_step(c, carry):
        r0 = pl.multiple_of(c * chunk, chunk)
        q = q_ref[pl.ds(r0, chunk), :]
        k = k_ref[pl.ds(r0, chunk), :]
        v = v_ref[pl.ds(r0, chunk), :]
        b = la_ref[pl.ds(r0, chunk), :]
        s = 1
        while s < chunk:
            b = b + jnp.where(row >= s, pltpu.roll(b, s, axis=0), 0.0)
            s *= 2
        qf = q * jnp.exp(b)
        kf = k * jnp.exp(-b)
        b_last = b[chunk - 1:chunk, :]
        k_tail = k * jnp.exp(b_last - b)
        for h in range(GLA_HEADS):
            ks = slice(h * GLA_DKH, (h + 1) * GLA_DKH)
            vs = slice(h * GLA_DVH, (h + 1) * GLA_DVH)
            qh = qf[:, ks].astype(BF16)
            kh = kf[:, ks].astype(BF16)
            vh = v[:, vs]
            state = so_ref[0, h]
            inter = _dot(qh, state.astype(BF16))
            att = jnp.where(causal, _dot_nt(qh, kh), 0.0).astype(BF16)
            o_scr[pl.ds(r0, chunk), vs] = inter + _dot(att, vh)
            decay = jnp.exp(b[:, ks].T[:, chunk - 1:chunk])
            so_ref[0, h] = state * decay + _dot(k_tail[:, ks].T.astype(BF16), vh)
        return carry

    lax.fori_loop(0, tile // chunk, chunk_step, 0)

    o = o_scr[...]
    parts = []
    for h in range(GLA_HEADS):
        oh = o[:, h * GLA_DVH:(h + 1) * GLA_DVH]
        parts.append(oh * _rms_scale(oh) * gn_ref[...])
    r = r_ref[...]
    gated = (jnp.concatenate(parts, axis=-1) * (r * (1.0 / (1.0 + jnp.exp(-r))))).astype(BF16)
    mix = _dot(gated, wo_ref[...])
    x1_ref[...] = _layer_norm(DEEPNORM_ALPHA * x_ref[...] + mix, lng_ref[...], lnb_ref[...])


def _gla_mix(q, k, la, v, r, x2d, state_in, gn, wo, lng, lnb, *, batch, seq, chunk):
    n = x2d.shape[0]
    tile = min(GLA_TILE, seq)
    nt = seq // tile
    row = lambda w: pl.BlockSpec((tile, w), lambda b, t: (b * nt + t, 0))
    st = pl.BlockSpec((1, GLA_HEADS, GLA_DKH, GLA_DVH), lambda b, t: (b, 0, 0, 0))
    return pl.pallas_call(
        functools.partial(_gla_kernel, tile=tile, chunk=chunk),
        grid=(batch, nt),
        in_specs=[row(GLA_DK), row(GLA_DK), row(GLA_DK), row(GLA_DV), row(GLA_DV), row(D_MODEL), st,
                  _resident(gn.shape), _resident(wo.shape), _resident(lng.shape), _resident(lnb.shape)],
        out_specs=[row(D_MODEL), st],
        out_shape=[jax.ShapeDtypeStruct((n, D_MODEL), F32),
                   jax.ShapeDtypeStruct((batch, GLA_HEADS, GLA_DKH, GLA_DVH), F32)],
        scratch_shapes=[pltpu.VMEM((tile, GLA_DV), F32)],
        compiler_params=_params(("parallel", "arbitrary")),
        name="gla_mix",
    )(q, k, la, v, r, x2d, state_in, gn, wo, lng, lnb)


def _mlp(xb, wup_ref, wdn_ref):
    acc = None
    for c in range(D_FF // FF_CHUNK):
        cs = slice(c * FF_CHUNK, (c + 1) * FF_CHUNK)
        hid = jnp.maximum(_dot(xb, wup_ref[:, cs]), 0.0)
        part = _dot((hid * hid).astype(BF16), wdn_ref[cs, :])
        acc = part if acc is None else acc + part
    return acc


def _rope(x, cos, sin):
    lane = lax.broadcasted_iota(jnp.int32, (x.shape[0], LANES), 1) & (DIFF_HD - 1)
    first = lane < ROPE_HALF
    second = (lane >= ROPE_HALF) & (lane < ROPE_DIMS)
    parts = []
    for j in range(x.shape[1] // LANES):
        xj = x[:, j * LANES:(j + 1) * LANES]
        ahead = pltpu.roll(xj, LANES - ROPE_HALF, axis=1)
        behind = pltpu.roll(xj, ROPE_HALF, axis=1)
        parts.append(jnp.where(first, xj * cos - ahead * sin,
                               jnp.where(second, xj * cos + behind * sin, xj)))
    return jnp.concatenate(parts, axis=-1)


def _rope_feature_major(xt, cos, sin):
    pieces = []
    for m in range(xt.shape[0] // DIFF_HD):
        base = m * DIFF_HD
        a = xt[base:base + ROPE_HALF]
        b = xt[base + ROPE_HALF:base + ROPE_DIMS]
        pieces += [a * cos - b * sin, b * cos + a * sin, xt[base + ROPE_DIMS:base + DIFF_HD]]
    return jnp.concatenate(pieces, axis=0)


def _mlp_kv_kernel(x1_ref, wup_ref, wdn_ref, lng_ref, lnb_ref, wk_ref, wv_ref, wq_ref, cos_ref, sin_ref,
                   kcos_ref, ksin_ref, x2_ref, k_ref, v_ref, q_ref, *, feature_major_k):
    x1 = x1_ref[...]
    y = _mlp(x1.astype(BF16), wup_ref, wdn_ref)
    x2 = _layer_norm(DEEPNORM_ALPHA * x1 + y, lng_ref[...], lnb_ref[...])
    x2_ref[...] = x2
    x2b = x2.astype(BF16)
    if feature_major_k:
        k_ref[0] = _rope_feature_major(_dot_nt(wk_ref[...], x2b), kcos_ref[...], ksin_ref[...])
    else:
        k_ref[...] = _rope(_dot(x2b, wk_ref[...]), kcos_ref[...], ksin_ref[...])
    v_ref[...] = _dot(x2b, wv_ref[...])
    q = _rope(_dot(x2b, wq_ref[...]), cos_ref[...], sin_ref[...])
    q_ref[...] = (q * (DIFF_HD ** -0.5 * LOG2E)).astype(BF16)


def _mlp_kv(x1, wup, wdn, lng, lnb, wk, wv, wq, tables, *, batch, seq, feature_major_k):
    n = x1.shape[0]
    tm = min(TOKEN_TILE, n)
    cos, sin, cos_t, sin_t = tables
    if tm > seq:
        cos = jnp.tile(cos, (tm // seq, 1))
        sin = jnp.tile(sin, (tm // seq, 1))
    npos = max(seq // tm, 1)
    row = lambda w: pl.BlockSpec((tm, w), lambda i: (i, 0))
    pos = pl.BlockSpec((tm, LANES), lambda i: (i % npos, 0))
    if feature_major_k:
        kcos, ksin = cos_t, sin_t
        kpos = pl.BlockSpec((ROPE_HALF, tm), lambda i: (0, i % npos))
        k_spec = pl.BlockSpec((1, D_MODEL, tm), lambda i: (i // npos, 0, i % npos))
        k_shape = jax.ShapeDtypeStruct((batch, D_MODEL, seq), F32)
    else:
        kcos, ksin, kpos = cos, sin, pos
        k_spec = row(D_MODEL)
        k_shape = jax.ShapeDtypeStruct((n, D_MODEL), F32)
    return pl.pallas_call(
        functools.partial(_mlp_kv_kernel, feature_major_k=feature_major_k),
        grid=(n // tm,),
        in_specs=[row(D_MODEL), _resident(wup.shape), _resident(wdn.shape), _resident(lng.shape),
                  _resident(lnb.shape), _resident(wk.shape), _resident(wv.shape), _resident(wq.shape),
                  pos, pos, kpos, kpos],
        out_specs=[row(D_MODEL), k_spec, row(D_MODEL), row(D_MODEL)],
        out_shape=[jax.ShapeDtypeStruct((n, D_MODEL), F32), k_shape,
                   jax.ShapeDtypeStruct((n, D_MODEL), F32), jax.ShapeDtypeStruct((n, D_MODEL), BF16)],
        compiler_params=_params(("parallel",)),
        name="mlp_kv",
    )(x1, wup, wdn, lng, lnb, wk, wv, wq, cos, sin, kcos, ksin)


def _out_mlp_kernel(o_ref, x2_ref, wo_ref, lng1_ref, lnb1_ref, wup_ref, wdn_ref, lng2_ref, lnb2_ref, y_ref):
    mix = _dot(o_ref[...], wo_ref[...])
    x3 = _layer_norm(DEEPNORM_ALPHA * x2_ref[...] + mix, lng1_ref[...], lnb1_ref[...])
    y = _mlp(x3.astype(BF16), wup_ref, wdn_ref)
    y_ref[...] = _layer_norm(DEEPNORM_ALPHA * x3 + y, lng2_ref[...], lnb2_ref[...])


def _out_mlp(o, x2, wo, lng1, lnb1, wup, wdn, lng2, lnb2):
    n = x2.shape[0]
    tm = min(TOKEN_TILE, n)
    row = lambda w: pl.BlockSpec((tm, w), lambda i: (i, 0))
    return pl.pallas_call(
        _out_mlp_kernel,
        grid=(n // tm,),
        in_specs=[row(D_MODEL), row(D_MODEL), _resident(wo.shape), _resident(lng1.shape),
                  _resident(lnb1.shape), _resident(wup.shape), _resident(wdn.shape),
                  _resident(lng2.shape), _resident(lnb2.shape)],
        out_specs=row(D_MODEL),
        out_shape=jax.ShapeDtypeStruct((n, D_MODEL), F32),
        compiler_params=_params(("parallel",)),
        name="out_mlp",
    )(o, x2, wo, lng1, lnb1, wup, wdn, lng2, lnb2)


def _lambda_full(lam):
    t1 = jnp.sum(lam[0:1, :] * lam[1:2, :], axis=-1, keepdims=True)
    t2 = jnp.sum(lam[2:3, :] * lam[3:4, :], axis=-1, keepdims=True)
    return jnp.exp(t1) - jnp.exp(t2) + LAMBDA_INIT


def _stack_maps(q):
    qf = q.astype(F32)
    lane = lax.broadcasted_iota(jnp.int32, qf.shape, 1)
    return jnp.concatenate([jnp.where(lane < DIFF_HD, qf, 0.0), jnp.where(lane >= DIFF_HD, qf, 0.0)],
                           axis=0).astype(BF16)


def _head_norm(acc, gn):
    return (acc * _rms_scale(acc) * gn * (1.0 - LAMBDA_INIT)).astype(BF16)


def _lane_fold(x, op):
    out = x[:, 0:LANES]
    for j in range(1, x.shape[1] // LANES):
        out = op(out, x[:, j * LANES:(j + 1) * LANES])
    return out


def _attn_kernel(lam_ref, gn_ref, q_ref, kt_ref, v_ref, o_ref, kb, vb, s_scr, *, tq, nq):
    kb[...] = kt_ref[0].astype(BF16)
    vb[...] = v_ref[0].astype(BF16)
    lam = _lambda_full(lam_ref[...])
    gn = gn_ref[...]
    chunk_shift = CHUNK.bit_length() - 1
    rc = (lax.broadcasted_iota(jnp.int32, (2 * tq, tq), 0) & (tq - 1)) >> chunk_shift
    cc = lax.broadcasted_iota(jnp.int32, (2 * tq, tq), 1) >> chunk_shift
    visible = cc <= rc
    for i in range(nq):
        slot = i % 2
        qq = _stack_maps(q_ref[0, i * tq:(i + 1) * tq, :])
        mfold = None
        for j in range(i + 1):
            s = _dot(qq, kb[:, j * tq:(j + 1) * tq])
            if j == i:
                s = jnp.where(visible, s, -jnp.inf)
            s_scr[slot, j] = s
            f = _lane_fold(s, jnp.maximum)
            mfold = f if mfold is None else jnp.maximum(mfold, f)
        m = jnp.max(mfold, axis=-1, keepdims=True)
        lfold = None
        for j in range(i + 1):
            e = jnp.exp2(s_scr[slot, j] - m)
            s_scr[slot, j] = e
            f = _lane_fold(e, jnp.add)
            lfold = f if lfold is None else lfold + f
        l = jnp.sum(lfold, axis=-1, keepdims=True)
        w0 = 1.0 / l[0:tq]
        w1 = lam / l[tq:]
        acc = None
        for j in range(i + 1):
            e = s_scr[slot, j]
            a = (e[0:tq] * w0 - e[tq:] * w1).astype(BF16)
            p = _dot(a, vb[j * tq:(j + 1) * tq, :])
            acc = p if acc is None else acc + p
        o_ref[0, i * tq:(i + 1) * tq, :] = _head_norm(acc, gn)


def _diff_attn_prompt(q, kt, v, lam, gn):
    batch, seq, _ = q.shape
    tq = min(ATT_TILE, seq)
    nq = seq // tq
    rows = pl.BlockSpec((1, seq, HEAD_W), lambda b, h: (b, 0, h))
    cols = pl.BlockSpec((1, HEAD_W, seq), lambda b, h: (b, h, 0))
    return pl.pallas_call(
        functools.partial(_attn_kernel, tq=tq, nq=nq),
        grid=(batch, DIFF_HEADS),
        in_specs=[_resident(lam.shape), _resident(gn.shape), rows, cols, rows],
        out_specs=rows,
        out_shape=jax.ShapeDtypeStruct((batch, seq, D_MODEL), BF16),
        scratch_shapes=[pltpu.VMEM((HEAD_W, seq), BF16), pltpu.VMEM((seq, HEAD_W), BF16),
                        pltpu.VMEM((2, nq, 2 * tq, tq), F32)],
        compiler_params=_params(("parallel", "parallel")),
        name="diff_attn_prompt",
    )(lam, gn, q, kt, v)


def _attn_sample_kernel(lam_ref, gn_ref, q_ref, ckt_ref, cv_ref, kn_ref, vn_ref, o_ref):
    lam = _lambda_full(lam_ref[...])
    gn = gn_ref[...]
    t = q_ref.shape[1]
    for h in range(DIFF_HEADS):
        hs = slice(h * HEAD_W, (h + 1) * HEAD_W)
        qq = _stack_maps(q_ref[0, :, hs])
        sp = _dot(qq, ckt_ref[0, hs, :].astype(BF16))
        sn = _dot_nt(qq, kn_ref[0, :, hs].astype(BF16))
        m = jnp.maximum(jnp.max(sp, axis=-1, keepdims=True), jnp.max(sn, axis=-1, keepdims=True))
        ep = jnp.exp2(sp - m)
        en = jnp.exp2(sn - m)
        l = jnp.sum(ep, axis=-1, keepdims=True) + jnp.sum(en, axis=-1, keepdims=True)
        w0 = 1.0 / l[0:t]
        w1 = lam / l[t:]
        ap = (ep[0:t] * w0 - ep[t:] * w1).astype(BF16)
        an = (en[0:t] * w0 - en[t:] * w1).astype(BF16)
        acc = _dot(ap, cv_ref[0, :, h, :].astype(BF16)) + _dot(an, vn_ref[0, :, hs].astype(BF16))
        o_ref[0, :, hs] = _head_norm(acc, gn)


def _diff_attn_sample(q, k_new, v_new, cache_kt, cache_v, lam, gn):
    batch, t, _ = q.shape
    past = cache_v.shape[1]
    new = pl.BlockSpec((1, t, D_MODEL), lambda b: (b, 0, 0))
    return pl.pallas_call(
        _attn_sample_kernel,
        grid=(batch,),
        in_specs=[_resident(lam.shape), _resident(gn.shape), new,
                  pl.BlockSpec((1, D_MODEL, past), lambda b: (b, 0, 0)),
                  pl.BlockSpec((1, past, DIFF_HEADS, HEAD_W), lambda b: (b, 0, 0, 0)), new, new],
        out_specs=new,
        out_shape=jax.ShapeDtypeStruct((batch, t, D_MODEL), BF16),
        compiler_params=_params(("parallel",)),
        name="diff_attn_sample",
    )(lam, gn, q, cache_kt, cache_v, k_new, v_new)


def _rope_tables(pos):
    inv = 1.0 / (ROPE_THETA ** (jnp.arange(0, ROPE_DIMS, 2, dtype=F32) / ROPE_DIMS))
    ang = pos.astype(F32)[:, None] * inv[None, :]
    cos, sin = jnp.cos(ang), jnp.sin(ang)
    idx = jnp.arange(LANES) % ROPE_HALF
    return cos[:, idx], sin[:, idx], cos.T, sin.T


def _prepare_weights(w_in_a, w_gate_up_a, b_gate_a, g_norm_a, w_o_a, w_kv, w_q_b, lam_b, g_norm_b, w_o_b,
                     w_up, w_down, ln_g, ln_b):
    n_main = 2 * GLA_DK + 2 * GLA_DV
    w_in = w_in_a[0]
    row = lambda a: a.reshape(1, -1)
    wk = w_kv[:, :D_MODEL].astype(BF16)
    return dict(
        wm=w_in[:, :n_main].astype(BF16),
        wgl=jnp.pad(w_in[:, n_main:], ((0, 0), (0, LANES - GATE_RANK))).astype(BF16),
        wg=jnp.pad(w_gate_up_a[0], ((0, LANES - GATE_RANK), (0, 0))).astype(BF16),
        bg=row(b_gate_a[0]),
        gn_a=row(g_norm_a[0]),
        wo_a=w_o_a[0].astype(BF16),
        wk=wk,
        wk_t=wk.T,
        wv=w_kv[:, D_MODEL:].astype(BF16),
        wq=w_q_b[0].astype(BF16),
        lam=lam_b[0],
        gn_b=row(g_norm_b[0]),
        wo_b=w_o_b[0].astype(BF16),
        wup=[w_up[i].astype(BF16) for i in range(DEPTH)],
        wdn=[w_down[i].astype(BF16) for i in range(DEPTH)],
        lng=[[row(ln_g[i, j]) for j in range(2)] for i in range(DEPTH)],
        lnb=[[row(ln_b[i, j]) for j in range(2)] for i in range(DEPTH)],
    )


def _trunk(x, pos, state_in, cache_k, cache_v, w):
    batch, seq, _ = x.shape
    n = batch * seq
    prompt = cache_k is None
    x2d = x.reshape(n, D_MODEL)
    q, k, v, r, la = _gla_inproj(x2d, w["wm"], w["wgl"], w["wg"], w["bg"])
    x1, state = _gla_mix(q, k, la, v, r, x2d, state_in, w["gn_a"], w["wo_a"], w["lng"][0][0], w["lnb"][0][0],
                         batch=batch, seq=seq, chunk=min(CHUNK, seq))
    x2, k_sh, v_sh, q_b = _mlp_kv(x1, w["wup"][0], w["wdn"][0], w["lng"][0][1], w["lnb"][0][1],
                                  w["wk_t"] if prompt else w["wk"], w["wv"], w["wq"], _rope_tables(pos),
                                  batch=batch, seq=seq, feature_major_k=prompt)
    v3 = v_sh.reshape(batch, seq, D_MODEL)
    q3 = q_b.reshape(batch, seq, D_MODEL)
    if prompt:
        o = _diff_attn_prompt(q3, k_sh, v3, w["lam"], w["gn_b"])
        k_out = k_sh.reshape(batch, DIFF_HEADS, 2, DIFF_HD, seq).transpose(0, 4, 1, 2, 3)
    else:
        past = cache_k.shape[1]
        cache_kt = cache_k.transpose(0, 2, 3, 4, 1).reshape(batch, D_MODEL, past)
        o = _diff_attn_sample(q3, k_sh.reshape(batch, seq, D_MODEL), v3, cache_kt, cache_v, w["lam"], w["gn_b"])
        k_out = k_sh.reshape(batch, seq, DIFF_HEADS, 2, DIFF_HD)
    y = _out_mlp(o.reshape(n, D_MODEL), x2, w["wo_b"], w["lng"][1][0], w["lnb"][1][0],
                 w["wup"][1], w["wdn"][1], w["lng"][1][1], w["lnb"][1][1])
    return (y.reshape(batch, seq, D_MODEL), state[None], k_out, v3.reshape(batch, seq, DIFF_HEADS, HEAD_W))


def kernel(x_prompt, x_sample, state_gla, cache_k, cache_v, w_in_a, w_gate_up_a, b_gate_a, g_norm_a, w_o_a, w_kv, w_q_b, lam_b, g_norm_b, w_o_b, w_up, w_down, ln_g, ln_b):
    w = _prepare_weights(w_in_a, w_gate_up_a, b_gate_a, g_norm_a, w_o_a, w_kv, w_q_b, lam_b, g_norm_b,
                         w_o_b, w_up, w_down, ln_g, ln_b)
    batch, seq, _ = x_prompt.shape
    past = cache_k.shape[1]
    zero_state = jnp.zeros((batch, GLA_HEADS, GLA_DKH, GLA_DVH), F32)
    y_p, s_p, k_p, v_p = _trunk(x_prompt, jnp.arange(seq), zero_state, None, None, w)
    y_s, s_s, k_s, v_s = _trunk(x_sample, past + jnp.arange(x_sample.shape[1]), state_gla[0],
                                cache_k, cache_v, w)
    return (y_p, y_s, s_p, k_p, v_p, s_s, k_s, v_s)
```

```python
import functools
import math

import jax
import jax.numpy as jnp
from jax import lax
from jax.experimental import pallas as pl
from jax.experimental.pallas import tpu as pltpu

F32 = jnp.float32
BF16 = jnp.bfloat16

D_MODEL = 1024
DEPTH = 2
CHUNK = 64
GLA_HEADS = 4
GLA_DK = D_MODEL // 2
GLA_DV = D_MODEL
GLA_DKH = GLA_DK // GLA_HEADS
GLA_DVH = GLA_DV // GLA_HEADS
GATE_RANK = 16
GATE_TAU = 16.0
DIFF_HEADS = 8
DIFF_HD = D_MODEL // (2 * DIFF_HEADS)
HEAD_W = 2 * DIFF_HD
ROPE_DIMS = DIFF_HD // 4
ROPE_HALF = ROPE_DIMS // 2
ROPE_THETA = 500000.0
D_FF = 4 * D_MODEL
DEEPNORM_ALPHA = (2 * DEPTH) ** 0.25
LN_EPS = 1e-5
RMS_EPS = 1e-5
LAMBDA_INIT = 0.8 - 0.6 * math.exp(-0.3 * 1)
LOG2E = math.log2(math.e)

LANES = 128
SUBLANES = 8
VMEM_LIMIT = 56 * 1024 * 1024
TOKEN_TILE = 512
GLA_TILE = 512
GLA_SAMPLE_ROWS = 128
ATT_TILE = 256
FF_CHUNK = 1024

assert ROPE_HALF == SUBLANES


def _dot(a, b):
    return jnp.dot(a, b, preferred_element_type=F32)


def _dot_nt(a, b):
    return lax.dot_general(a, b, (((1,), (1,)), ((), ())), preferred_element_type=F32)


def _layer_norm(z, g, b):
    mu = jnp.mean(z, axis=-1, keepdims=True)
    zc = z - mu
    var = jnp.mean(zc * zc, axis=-1, keepdims=True)
    return zc * lax.rsqrt(var + LN_EPS) * g + b


def _rms_scale(x):
    return lax.rsqrt(jnp.mean(x * x, axis=-1, keepdims=True) + RMS_EPS)


def _resident(shape):
    zeros = (0,) * len(shape)
    return pl.BlockSpec(shape, lambda *_: zeros, pipeline_mode=pl.Buffered(1))


def _params(semantics):
    return pltpu.CompilerParams(dimension_semantics=semantics, vmem_limit_bytes=VMEM_LIMIT)


def _gla_layer_kernel(x_ref, si_ref, wm_ref, wgl_ref, wg_ref, bg_ref, gn_ref, wo_ref, lng_ref, lnb_ref,
                      x1_ref, so_ref, qf_scr, kf_scr, k_scr, b_scr, v_scr, r_scr, o_scr, *, nb, tile, chunk):
    @pl.when(pl.program_id(1) == 0)
    def _():
        so_ref[...] = si_ref[...]

    xb = x_ref[...].astype(BF16)
    q = _dot(xb, wm_ref[:, 0:GLA_DK]) * (GLA_DKH ** -0.5)
    k = _dot(xb, wm_ref[:, GLA_DK:2 * GLA_DK])
    v_scr[...] = _dot(xb, wm_ref[:, 2 * GLA_DK:2 * GLA_DK + GLA_DV]).astype(BF16)
    r_scr[...] = _dot(xb, wm_ref[:, 2 * GLA_DK + GLA_DV:])
    gl = _dot(xb, wgl_ref[...]).astype(BF16)
    logit = _dot(gl, wg_ref[...]) + bg_ref[...]
    log_sig = jnp.minimum(logit, 0.0) - jnp.log1p(jnp.exp(-jnp.abs(logit)))
    b = log_sig * (1.0 / GATE_TAU)
    row_in_chunk = lax.broadcasted_iota(jnp.int32, b.shape, 0) & (chunk - 1)
    s = 1
    while s < chunk:
        b = b + jnp.where(row_in_chunk >= s, pltpu.roll(b, s, axis=0), 0.0)
        s *= 2
    qf_scr[...] = (q * jnp.exp(b)).astype(BF16)
    kf_scr[...] = (k * jnp.exp(-b)).astype(BF16)
    k_scr[...] = k
    b_scr[...] = b

    ci = lax.broadcasted_iota(jnp.int32, (chunk, chunk), 0)
    cj = lax.broadcasted_iota(jnp.int32, (chunk, chunk), 1)
    causal = cj <= ci
    items = [(bi, c, h) for bi in range(nb) for c in range(tile // chunk) for h in range(GLA_HEADS)]
    chunk_vals = {}
    live = {}

    def chunk_prelude(bi, c):
        rows = slice(bi * tile + c * chunk, bi * tile + (c + 1) * chunk)
        b_c = b_scr[rows, :]
        b_last = b_c[chunk - 1:chunk, :]
        k_tail = (k_scr[rows, :] * jnp.exp(b_last - b_c)).astype(BF16)
        decay_rows = jnp.broadcast_to(jnp.exp(b_last), (SUBLANES, GLA_DK))
        return rows, k_tail, decay_rows

    def stage_scores(n):
        bi, c, h = items[n]
        if h == 0:
            chunk_vals[(bi, c)] = chunk_prelude(bi, c)
        rows, k_tail, decay_rows = chunk_vals[(bi, c)]
        ks = slice(h * GLA_DKH, (h + 1) * GLA_DKH)
        vs = slice(h * GLA_DVH, (h + 1) * GLA_DVH)
        qh = qf_scr[rows, ks]
        vh = v_scr[rows, vs]
        live[n] = dict(
            rows=rows, vs=vs, qh=qh, vh=vh,
            scores=_dot_nt(qh, kf_scr[rows, ks]),
            update=lax.dot_general(k_tail[:, ks], vh, (((0,), (0,)), ((), ())), preferred_element_type=F32),
            decay=decay_rows[:, ks].T[:, 0:1])

    def stage_outputs(n):
        bi, _, h = items[n]
        it = live[n]
        it["state"] = so_ref[bi, h]
        att = jnp.where(causal, it["scores"], 0.0).astype(BF16)
        it["o"] = _dot(it["qh"], it["state"].astype(BF16)) + _dot(att, it["vh"])

    def stage_commit(n):
        bi, _, h = items[n]
        it = live.pop(n)
        o_scr[it["rows"], it["vs"]] = it["o"]
        so_ref[bi, h] = it["state"] * it["decay"] + it["update"]

    stages = (stage_scores, stage_outputs, stage_commit)
    for step in range(len(items) + len(stages) - 1):
        for depth, stage in enumerate(stages):
            n = step - depth
            if 0 <= n < len(items):
                stage(n)

    o = o_scr[...]
    parts = []
    for h in range(GLA_HEADS):
        oh = o[:, h * GLA_DVH:(h + 1) * GLA_DVH]
        parts.append(oh * _rms_scale(oh) * gn_ref[...])
    r = r_scr[...]
    gated = (jnp.concatenate(parts, axis=-1) * (r * (1.0 / (1.0 + jnp.exp(-r))))).astype(BF16)
    mix = _dot(gated, wo_ref[...])
    x1_ref[...] = _layer_norm(DEEPNORM_ALPHA * x_ref[...] + mix, lng_ref[...], lnb_ref[...])


def _gla_layer(x2d, state_in, wm, wgl, wg, bg, gn, wo, lng, lnb, *, batch, seq, chunk):
    n = x2d.shape[0]
    tile = min(GLA_TILE, seq)
    nt = seq // tile
    nb = max(1, min(GLA_SAMPLE_ROWS // seq, batch)) if tile == seq else 1
    m = nb * tile
    row = pl.BlockSpec((m, D_MODEL), lambda g, t: (g * nt + t, 0))
    st = pl.BlockSpec((nb, GLA_HEADS, GLA_DKH, GLA_DVH), lambda g, t: (g, 0, 0, 0))
    return pl.pallas_call(
        functools.partial(_gla_layer_kernel, nb=nb, tile=tile, chunk=chunk),
        grid=(batch // nb, nt),
        in_specs=[row, st, _resident(wm.shape), _resident(wgl.shape), _resident(wg.shape), _resident(bg.shape),
                  _resident(gn.shape), _resident(wo.shape), _resident(lng.shape), _resident(lnb.shape)],
        out_specs=[row, st],
        out_shape=[jax.ShapeDtypeStruct((n, D_MODEL), F32),
                   jax.ShapeDtypeStruct((batch, GLA_HEADS, GLA_DKH, GLA_DVH), F32)],
        scratch_shapes=[pltpu.VMEM((m, GLA_DK), BF16), pltpu.VMEM((m, GLA_DK), BF16),
                        pltpu.VMEM((m, GLA_DK), F32), pltpu.VMEM((m, GLA_DK), F32),
                        pltpu.VMEM((m, GLA_DV), BF16), pltpu.VMEM((m, GLA_DV), F32),
                        pltpu.VMEM((m, GLA_DV), F32)],
        compiler_params=_params(("parallel", "arbitrary")),
        name="gla_layer",
    )(x2d, state_in, wm, wgl, wg, bg, gn, wo, lng, lnb)


def _mlp(xb, wup_ref, wdn_ref):
    acc = None
    for c in range(D_FF // FF_CHUNK):
        cs = slice(c * FF_CHUNK, (c + 1) * FF_CHUNK)
        hid = jnp.maximum(_dot(xb, wup_ref[:, cs]), 0.0)
        part = _dot((hid * hid).astype(BF16), wdn_ref[cs, :])
        acc = part if acc is None else acc + part
    return acc


def _rope(x, cos, sin):
    lane = lax.broadcasted_iota(jnp.int32, (x.shape[0], LANES), 1) & (DIFF_HD - 1)
    first = lane < ROPE_HALF
    second = (lane >= ROPE_HALF) & (lane < ROPE_DIMS)
    parts = []
    for j in range(x.shape[1] // LANES):
        xj = x[:, j * LANES:(j + 1) * LANES]
        ahead = pltpu.roll(xj, LANES - ROPE_HALF, axis=1)
        behind = pltpu.roll(xj, ROPE_HALF, axis=1)
        parts.append(jnp.where(first, xj * cos - ahead * sin,
                               jnp.where(second, xj * cos + behind * sin, xj)))
    return jnp.concatenate(parts, axis=-1)


def _rope_feature_major(xt, cos, sin):
    pieces = []
    for m in range(xt.shape[0] // DIFF_HD):
        base = m * DIFF_HD
        a = xt[base:base + ROPE_HALF]
        b = xt[base + ROPE_HALF:base + ROPE_DIMS]
        pieces += [a * cos - b * sin, b * cos + a * sin, xt[base + ROPE_DIMS:base + DIFF_HD]]
    return jnp.concatenate(pieces, axis=0)


def _mlp_kv_kernel(x1_ref, wup_ref, wdn_ref, lng_ref, lnb_ref, wk_ref, wv_ref, wq_ref, cos_ref, sin_ref,
                   kcos_ref, ksin_ref, x2_ref, k_ref, v_ref, q_ref, *, feature_major_k):
    x1 = x1_ref[...]
    y = _mlp(x1.astype(BF16), wup_ref, wdn_ref)
    x2 = _layer_norm(DEEPNORM_ALPHA * x1 + y, lng_ref[...], lnb_ref[...])
    x2_ref[...] = x2
    x2b = x2.astype(BF16)
    if feature_major_k:
        k_ref[0] = _rope_feature_major(_dot_nt(wk_ref[...], x2b), kcos_ref[...], ksin_ref[...])
    else:
        k_ref[...] = _rope(_dot(x2b, wk_ref[...]), kcos_ref[...], ksin_ref[...])
    v_ref[...] = _dot(x2b, wv_ref[...])
    q = _rope(_dot(x2b, wq_ref[...]), cos_ref[...], sin_ref[...])
    q_ref[...] = (q * (DIFF_HD ** -0.5 * LOG2E)).astype(BF16)


def _mlp_kv(x1, wup, wdn, lng, lnb, wk, wv, wq, tables, *, batch, seq, feature_major_k):
    n = x1.shape[0]
    tm = min(TOKEN_TILE, n)
    cos, sin, cos_t, sin_t = tables
    if tm > seq:
        cos = jnp.tile(cos, (tm // seq, 1))
        sin = jnp.tile(sin, (tm // seq, 1))
    npos = max(seq // tm, 1)
    row = lambda w: pl.BlockSpec((tm, w), lambda i: (i, 0))
    pos = pl.BlockSpec((tm, LANES), lambda i: (i % npos, 0))
    if feature_major_k:
        kcos, ksin = cos_t, sin_t
        kpos = pl.BlockSpec((ROPE_HALF, tm), lambda i: (0, i % npos))
        k_spec = pl.BlockSpec((1, D_MODEL, tm), lambda i: (i // npos, 0, i % npos))
        k_shape = jax.ShapeDtypeStruct((batch, D_MODEL, seq), F32)
    else:
        kcos, ksin, kpos = cos, sin, pos
        k_spec = row(D_MODEL)
        k_shape = jax.ShapeDtypeStruct((n, D_MODEL), F32)
    return pl.pallas_call(
        functools.partial(_mlp_kv_kernel, feature_major_k=feature_major_k),
        grid=(n // tm,),
        in_specs=[row(D_MODEL), _resident(wup.shape), _resident(wdn.shape), _resident(lng.shape),
                  _resident(lnb.shape), _resident(wk.shape), _resident(wv.shape), _resident(wq.shape),
                  pos, pos, kpos, kpos],
        out_specs=[row(D_MODEL), k_spec, row(D_MODEL), row(D_MODEL)],
        out_shape=[jax.ShapeDtypeStruct((n, D_MODEL), F32), k_shape,
                   jax.ShapeDtypeStruct((n, D_MODEL), F32), jax.ShapeDtypeStruct((n, D_MODEL), BF16)],
        compiler_params=_params(("parallel",)),
        name="mlp_kv",
    )(x1, wup, wdn, lng, lnb, wk, wv, wq, cos, sin, kcos, ksin)


def _out_mlp_kernel(o_ref, x2_ref, wo_ref, lng1_ref, lnb1_ref, wup_ref, wdn_ref, lng2_ref, lnb2_ref, y_ref):
    mix = _dot(o_ref[...], wo_ref[...])
    x3 = _layer_norm(DEEPNORM_ALPHA * x2_ref[...] + mix, lng1_ref[...], lnb1_ref[...])
    y = _mlp(x3.astype(BF16), wup_ref, wdn_ref)
    y_ref[...] = _layer_norm(DEEPNORM_ALPHA * x3 + y, lng2_ref[...], lnb2_ref[...])


def _out_mlp(o, x2, wo, lng1, lnb1, wup, wdn, lng2, lnb2):
    n = x2.shape[0]
    tm = min(TOKEN_TILE, n)
    row = lambda w: pl.BlockSpec((tm, w), lambda i: (i, 0))
    return pl.pallas_call(
        _out_mlp_kernel,
        grid=(n // tm,),
        in_specs=[row(D_MODEL), row(D_MODEL), _resident(wo.shape), _resident(lng1.shape),
                  _resident(lnb1.shape), _resident(wup.shape), _resident(wdn.shape),
                  _resident(lng2.shape), _resident(lnb2.shape)],
        out_specs=row(D_MODEL),
        out_shape=jax.ShapeDtypeStruct((n, D_MODEL), F32),
        compiler_params=_params(("parallel",)),
        name="out_mlp",
    )(o, x2, wo, lng1, lnb1, wup, wdn, lng2, lnb2)


def _lambda_full(lam):
    t1 = jnp.sum(lam[0:1, :] * lam[1:2, :], axis=-1, keepdims=True)
    t2 = jnp.sum(lam[2:3, :] * lam[3:4, :], axis=-1, keepdims=True)
    return jnp.exp(t1) - jnp.exp(t2) + LAMBDA_INIT


def _stack_maps(q):
    qf = q.astype(F32)
    lane = lax.broadcasted_iota(jnp.int32, qf.shape, 1)
    return jnp.concatenate([jnp.where(lane < DIFF_HD, qf, 0.0), jnp.where(lane >= DIFF_HD, qf, 0.0)],
                           axis=0).astype(BF16)


def _head_norm(acc, gn):
    return (acc * _rms_scale(acc) * gn * (1.0 - LAMBDA_INIT)).astype(BF16)


def _lane_fold(x, op):
    out = x[:, 0:LANES]
    for j in range(1, x.shape[1] // LANES):
        out = op(out, x[:, j * LANES:(j + 1) * LANES])
    return out


def _interleave(a, b):
    out = []
    ia = ib = 0
    while ia < len(a) or ib < len(b):
        if ib >= len(b) or (ia < len(a) and ia * len(b) <= ib * len(a)):
            out.append(a[ia])
            ia += 1
        else:
            out.append(b[ib])
            ib += 1
    return out


def _attn_kernel(lam_ref, gn_ref, q_ref, kt_ref, v_ref, o_ref, kb, vb, s_scr, *, tq, nq):
    kb[...] = kt_ref[0].astype(BF16)
    vb[...] = v_ref[0].astype(BF16)
    lam = _lambda_full(lam_ref[...])
    gn = gn_ref[...]
    chunk_shift = CHUNK.bit_length() - 1
    rc = (lax.broadcasted_iota(jnp.int32, (2 * tq, tq), 0) & (tq - 1)) >> chunk_shift
    cc = lax.broadcasted_iota(jnp.int32, (2 * tq, tq), 1) >> chunk_shift
    visible = cc <= rc

    def score_units(i):
        st = {}
        slot = i % 2

        def start():
            st["qq"] = _stack_maps(q_ref[0, i * tq:(i + 1) * tq, :])

        def block(j):
            def run():
                s = _dot(st["qq"], kb[:, j * tq:(j + 1) * tq])
                if j == i:
                    s = jnp.where(visible, s, -jnp.inf)
                s_scr[slot, j] = s
                f = _lane_fold(s, jnp.maximum)
                st["m"] = f if j == 0 else jnp.maximum(st["m"], f)
            return run

        def finish():
            st["m"] = jnp.max(st["m"], axis=-1, keepdims=True)

        return [start] + [block(j) for j in range(i + 1)] + [finish], st

    def softmax_units(i, st):
        slot = i % 2

        def exp_block(j):
            def run():
                e = jnp.exp2(s_scr[slot, j] - st["m"])
                s_scr[slot, j] = e
                f = _lane_fold(e, jnp.add)
                st["l"] = f if j == 0 else st["l"] + f
            return run

        def weights():
            l = jnp.sum(st["l"], axis=-1, keepdims=True)
            st["inv_l0"] = 1.0 / l[0:tq]
            st["rho"] = lam * l[0:tq] / l[tq:]

        def pv_block(j):
            def run():
                e = s_scr[slot, j]
                a = (e[0:tq] - e[tq:] * st["rho"]).astype(BF16)
                p = _dot(a, vb[j * tq:(j + 1) * tq, :])
                st["acc"] = p if j == 0 else st["acc"] + p
            return run

        def store():
            o_ref[0, i * tq:(i + 1) * tq, :] = _head_norm(st["acc"] * st["inv_l0"], gn)

        return ([exp_block(j) for j in range(i + 1)] + [weights]
                + [pv_block(j) for j in range(i + 1)] + [store])

    units, st = score_units(0)
    for unit in units:
        unit()
    for i in range(nq):
        ahead, st_next = score_units(i + 1) if i + 1 < nq else ([], None)
        for unit in _interleave(softmax_units(i, st), ahead):
            unit()
        st = st_next


def _diff_attn_prompt(q, kt, v, lam, gn):
    batch, seq, _ = q.shape
    tq = min(ATT_TILE, seq)
    nq = seq // tq
    rows = pl.BlockSpec((1, seq, HEAD_W), lambda b, h: (b, 0, h))
    cols = pl.BlockSpec((1, HEAD_W, seq), lambda b, h: (b, h, 0))
    return pl.pallas_call(
        functools.partial(_attn_kernel, tq=tq, nq=nq),
        grid=(batch, DIFF_HEADS),
        in_specs=[_resident(lam.shape), _resident(gn.shape), rows, cols, rows],
        out_specs=rows,
        out_shape=jax.ShapeDtypeStruct((batch, seq, D_MODEL), BF16),
        scratch_shapes=[pltpu.VMEM((HEAD_W, seq), BF16), pltpu.VMEM((seq, HEAD_W), BF16),
                        pltpu.VMEM((2, nq, 2 * tq, tq), F32)],
        compiler_params=_params(("parallel", "parallel")),
        name="diff_attn_prompt",
    )(lam, gn, q, kt, v)


def _attn_sample_kernel(lam_ref, gn_ref, q_ref, ckt_ref, cv_ref, kn_ref, vn_ref, o_ref):
    lam = _lambda_full(lam_ref[...])
    gn = gn_ref[...]
    t = q_ref.shape[1]
    for h in range(DIFF_HEADS):
        hs = slice(h * HEAD_W, (h + 1) * HEAD_W)
        qq = _stack_maps(q_ref[0, :, hs])
        sp = _dot(qq, ckt_ref[0, hs, :].astype(BF16))
        sn = _dot_nt(qq, kn_ref[0, :, hs].astype(BF16))
        m = jnp.maximum(jnp.max(sp, axis=-1, keepdims=True), jnp.max(sn, axis=-1, keepdims=True))
        ep = jnp.exp2(sp - m)
        en = jnp.exp2(sn - m)
        l = jnp.sum(ep, axis=-1, keepdims=True) + jnp.sum(en, axis=-1, keepdims=True)
        rho = lam * l[0:t] / l[t:]
        ap = (ep[0:t] - ep[t:] * rho).astype(BF16)
        an = (en[0:t] - en[t:] * rho).astype(BF16)
        acc = _dot(ap, cv_ref[0, :, h, :].astype(BF16)) + _dot(an, vn_ref[0, :, hs].astype(BF16))
        o_ref[0, :, hs] = _head_norm(acc * (1.0 / l[0:t]), gn)


def _diff_attn_sample(q, k_new, v_new, cache_kt, cache_v, lam, gn):
    batch, t, _ = q.shape
    past = cache_v.shape[1]
    new = pl.BlockSpec((1, t, D_MODEL), lambda b: (b, 0, 0))
    return pl.pallas_call(
        _attn_sample_kernel,
        grid=(batch,),
        in_specs=[_resident(lam.shape), _resident(gn.shape), new,
                  pl.BlockSpec((1, D_MODEL, past), lambda b: (b, 0, 0)),
                  pl.BlockSpec((1, past, DIFF_HEADS, HEAD_W), lambda b: (b, 0, 0, 0)), new, new],
        out_specs=new,
        out_shape=jax.ShapeDtypeStruct((batch, t, D_MODEL), BF16),
        compiler_params=_params(("parallel",)),
        name="diff_attn_sample",
    )(lam, gn, q, cache_kt, cache_v, k_new, v_new)


def _rope_tables(pos):
    inv = 1.0 / (ROPE_THETA ** (jnp.arange(0, ROPE_DIMS, 2, dtype=F32) / ROPE_DIMS))
    ang = pos.astype(F32)[:, None] * inv[None, :]
    cos, sin = jnp.cos(ang), jnp.sin(ang)
    idx = jnp.arange(LANES) % ROPE_HALF
    return cos[:, idx], sin[:, idx], cos.T, sin.T


def _prepare_weights(w_in_a, w_gate_up_a, b_gate_a, g_norm_a, w_o_a, w_kv, w_q_b, lam_b, g_norm_b, w_o_b,
                     w_up, w_down, ln_g, ln_b):
    n_main = 2 * GLA_DK + 2 * GLA_DV
    w_in = w_in_a[0]
    row = lambda a: a.reshape(1, -1)
    wk = w_kv[:, :D_MODEL].astype(BF16)
    return dict(
        wm=w_in[:, :n_main].astype(BF16),
        wgl=jnp.pad(w_in[:, n_main:], ((0, 0), (0, LANES - GATE_RANK))).astype(BF16),
        wg=jnp.pad(w_gate_up_a[0], ((0, LANES - GATE_RANK), (0, 0))).astype(BF16),
        bg=row(b_gate_a[0]),
        gn_a=row(g_norm_a[0]),
        wo_a=w_o_a[0].astype(BF16),
        wk=wk,
        wk_t=wk.T,
        wv=w_kv[:, D_MODEL:].astype(BF16),
        wq=w_q_b[0].astype(BF16),
        lam=lam_b[0],
        gn_b=row(g_norm_b[0]),
        wo_b=w_o_b[0].astype(BF16),
        wup=[w_up[i].astype(BF16) for i in range(DEPTH)],
        wdn=[w_down[i].astype(BF16) for i in range(DEPTH)],
        lng=[[row(ln_g[i, j]) for j in range(2)] for i in range(DEPTH)],
        lnb=[[row(ln_b[i, j]) for j in range(2)] for i in range(DEPTH)],
    )


def _trunk(x, pos, state_in, cache_k, cache_v, w):
    batch, seq, _ = x.shape
    n = batch * seq
    prompt = cache_k is None
    x2d = x.reshape(n, D_MODEL)
    x1, state = _gla_layer(x2d, state_in, w["wm"], w["wgl"], w["wg"], w["bg"], w["gn_a"], w["wo_a"],
                           w["lng"][0][0], w["lnb"][0][0], batch=batch, seq=seq, chunk=min(CHUNK, seq))
    x2, k_sh, v_sh, q_b = _mlp_kv(x1, w["wup"][0], w["wdn"][0], w["lng"][0][1], w["lnb"][0][1],
                                  w["wk_t"] if prompt else w["wk"], w["wv"], w["wq"], _rope_tables(pos),
                                  batch=batch, seq=seq, feature_major_k=prompt)
    v3 = v_sh.reshape(batch, seq, D_MODEL)
    q3 = q_b.reshape(batch, seq, D_MODEL)
    if prompt:
        o = _diff_attn_prompt(q3, k_sh, v3, w["lam"], w["gn_b"])
        k_out = k_sh.reshape(batch, DIFF_HEADS, 2, DIFF_HD, seq).transpose(0, 4, 1, 2, 3)
    else:
        past = cache_k.shape[1]
        cache_kt = cache_k.transpose(0, 2, 3, 4, 1).reshape(batch, D_MODEL, past)
        o = _diff_attn_sample(q3, k_sh.reshape(batch, seq, D_MODEL), v3, cache_kt, cache_v, w["lam"], w["gn_b"])
        k_out = k_sh.reshape(batch, seq, DIFF_HEADS, 2, DIFF_HD)
    y = _out_mlp(o.reshape(n, D_MODEL), x2, w["wo_b"], w["lng"][1][0], w["lnb"][1][0],
                 w["wup"][1], w["wdn"][1], w["lng"][1][1], w["lnb"][1][1])
    return (y.reshape(batch, seq, D_MODEL), state[None], k_out, v3.reshape(batch, seq, DIFF_HEADS, HEAD_W))


def kernel(x_prompt, x_sample, state_gla, cache_k, cache_v, w_in_a, w_gate_up_a, b_gate_a, g_norm_a, w_o_a, w_kv, w_q_b, lam_b, g_norm_b, w_o_b, w_up, w_down, ln_g, ln_b):
    w = _prepare_weights(w_in_a, w_gate_up_a, b_gate_a, g_norm_a, w_o_a, w_kv, w_q_b, lam_b, g_norm_b,
                         w_o_b, w_up, w_down, ln_g, ln_b)
    batch, seq, _ = x_prompt.shape
    past = cache_k.shape[1]
    zero_state = jnp.zeros((batch, GLA_HEADS, GLA_DKH, GLA_DVH), F32)
    y_p, s_p, k_p, v_p = _trunk(x_prompt, jnp.arange(seq), zero_state, None, None, w)
    y_s, s_s, k_s, v_s = _trunk(x_sample, past + jnp.arange(x_sample.shape[1]), state_gla[0],
                                cache_k, cache_v, w)
    return (y_p, y_s, s_p, k_p, v_p, s_s, k_s, v_s)
```

```python
import functools
import math

import jax
import jax.numpy as jnp
from jax import lax
from jax.experimental import pallas as pl
from jax.experimental.pallas import tpu as pltpu

F32 = jnp.float32
BF16 = jnp.bfloat16

D_MODEL = 1024
DEPTH = 2
CHUNK = 64
GLA_HEADS = 4
GLA_DK = D_MODEL // 2
GLA_DV = D_MODEL
GLA_DKH = GLA_DK // GLA_HEADS
GLA_DVH = GLA_DV // GLA_HEADS
GATE_RANK = 16
GATE_TAU = 16.0
DIFF_HEADS = 8
DIFF_HD = D_MODEL // (2 * DIFF_HEADS)
HEAD_W = 2 * DIFF_HD
ROPE_DIMS = DIFF_HD // 4
ROPE_HALF = ROPE_DIMS // 2
ROPE_THETA = 500000.0
D_FF = 4 * D_MODEL
DEEPNORM_ALPHA = (2 * DEPTH) ** 0.25
LN_EPS = 1e-5
RMS_EPS = 1e-5
LAMBDA_INIT = 0.8 - 0.6 * math.exp(-0.3 * 1)
LOG2E = math.log2(math.e)

LANES = 128
SUBLANES = 8
MXU_COLS = 256
VMEM_LIMIT = 56 * 1024 * 1024
TOKEN_TILE = 512
GLA_TILE = 512
GLA_SAMPLE_ROWS = 128
ATT_TILE = 256
FF_CHUNK = 1024

assert ROPE_HALF == SUBLANES


def _dot(a, b):
    return jnp.dot(a, b, preferred_element_type=F32)


def _dot_nt(a, b):
    return lax.dot_general(a, b, (((1,), (1,)), ((), ())), preferred_element_type=F32)


def _layer_norm(z, g, b):
    mu = jnp.mean(z, axis=-1, keepdims=True)
    zc = z - mu
    var = jnp.mean(zc * zc, axis=-1, keepdims=True)
    return zc * lax.rsqrt(var + LN_EPS) * g + b


def _rms_scale(x):
    return lax.rsqrt(jnp.mean(x * x, axis=-1, keepdims=True) + RMS_EPS)


def _resident(shape):
    zeros = (0,) * len(shape)
    return pl.BlockSpec(shape, lambda *_: zeros, pipeline_mode=pl.Buffered(1))


def _interleave(a, b):
    out = []
    ia = ib = 0
    while ia < len(a) or ib < len(b):
        if ib >= len(b) or (ia < len(a) and ia * len(b) <= ib * len(a)):
            out.append(a[ia])
            ia += 1
        else:
            out.append(b[ib])
            ib += 1
    return out


def _params(semantics):
    return pltpu.CompilerParams(dimension_semantics=semantics, vmem_limit_bytes=VMEM_LIMIT)


def _gla_layer_kernel(x_ref, si_ref, wm_ref, wgl_ref, wg_ref, bg_ref, gn_ref, wo_ref, lng_ref, lnb_ref,
                      x1_ref, so_ref, qf_scr, kf_scr, k_scr, b_scr, v_scr, r_scr, o_scr, *, nb, tile, chunk):
    @pl.when(pl.program_id(1) == 0)
    def _():
        so_ref[...] = si_ref[...]

    ci = lax.broadcasted_iota(jnp.int32, (chunk, chunk), 0)
    cj = lax.broadcasted_iota(jnp.int32, (chunk, chunk), 1)
    causal = cj <= ci
    half = nb * tile // 2
    row_in_chunk = lax.broadcasted_iota(jnp.int32, (half, MXU_COLS), 0) & (chunk - 1)

    def projection_units(rs):
        st = {}

        def cast():
            st["xb"] = x_ref[rs, :].astype(BF16)

        def gate_down():
            st["gl"] = _dot(st["xb"], wgl_ref[...]).astype(BF16)

        def gate_up(c0):
            def run():
                logit = _dot(st["gl"], wg_ref[:, c0:c0 + MXU_COLS]) + bg_ref[:, c0:c0 + MXU_COLS]
                log_sig = jnp.minimum(logit, 0.0) - jnp.log(1.0 + jnp.exp(-jnp.abs(logit)))
                st.setdefault("b", []).append(log_sig * (1.0 / GATE_TAU))
            return run

        def proj(c0):
            def run():
                y = _dot(st["xb"], wm_ref[:, c0:c0 + MXU_COLS])
                if c0 < GLA_DK:
                    st.setdefault("q", []).append(y * (GLA_DKH ** -0.5))
                elif c0 < 2 * GLA_DK:
                    st.setdefault("k", []).append(y)
                elif c0 < 2 * GLA_DK + GLA_DV:
                    v_scr[rs, c0 - 2 * GLA_DK:c0 - 2 * GLA_DK + MXU_COLS] = y.astype(BF16)
                else:
                    c = c0 - 2 * GLA_DK - GLA_DV
                    r_scr[rs, c:c + MXU_COLS] = y
            return run

        n_main = 2 * GLA_DK + 2 * GLA_DV
        matmuls = [proj(c0) for c0 in range(0, n_main, MXU_COLS)]
        gates = [gate_up(c0) for c0 in range(0, GLA_DK, MXU_COLS)]
        return [cast, gate_down] + _interleave(matmuls, gates), st

    def decay_units(rs, st):
        def scan_step(g, s):
            def run():
                b = st["b"][g]
                st["b"][g] = b + jnp.where(row_in_chunk >= s, pltpu.roll(b, s, axis=0), 0.0)
            return run

        def fold(g):
            def run():
                cols = slice(g * MXU_COLS, (g + 1) * MXU_COLS)
                q, k, b = st["q"][g], st["k"][g], st["b"][g]
                qf_scr[rs, cols] = (q * jnp.exp(b)).astype(BF16)
                kf_scr[rs, cols] = (k * jnp.exp(-b)).astype(BF16)
                k_scr[rs, cols] = k
                b_scr[rs, cols] = b
            return run

        shifts = [1 << p for p in range(chunk.bit_length() - 1)]
        groups = range(GLA_DK // MXU_COLS)
        return [scan_step(g, s) for s in shifts for g in groups] + [fold(g) for g in groups]

    def recurrence_units(row0):
        seqs = range(row0 // tile, (row0 + half - 1) // tile + 1)
        items = [(bi, r0, h) for bi in seqs
                 for r0 in range(max(row0, bi * tile), min(row0 + half, (bi + 1) * tile), chunk)
                 for h in range(GLA_HEADS)]
        chunk_vals = {}
        live = {}

        def chunk_prelude(r0):
            rows = slice(r0, r0 + chunk)
            b_c = b_scr[rows, :]
            b_last = b_c[chunk - 1:chunk, :]
            k_tail = (k_scr[rows, :] * jnp.exp(b_last - b_c)).astype(BF16)
            decay_rows = jnp.broadcast_to(jnp.exp(b_last), (SUBLANES, GLA_DK))
            return rows, k_tail, decay_rows

        def stage_scores(n):
            _, r0, h = items[n]
            if h == 0:
                chunk_vals[r0] = chunk_prelude(r0)
            rows, k_tail, decay_rows = chunk_vals[r0]
            ks = slice(h * GLA_DKH, (h + 1) * GLA_DKH)
            vs = slice(h * GLA_DVH, (h + 1) * GLA_DVH)
            qh = qf_scr[rows, ks]
            vh = v_scr[rows, vs]
            live[n] = dict(
                rows=rows, vs=vs, qh=qh, vh=vh,
                scores=_dot_nt(qh, kf_scr[rows, ks]),
                update=lax.dot_general(k_tail[:, ks], vh, (((0,), (0,)), ((), ())),
                                       preferred_element_type=F32),
                decay=decay_rows[:, ks].T[:, 0:1])

        def stage_outputs(n):
            bi, _, h = items[n]
            it = live[n]
            it["state"] = so_ref[bi, h]
            att = jnp.where(causal, it["scores"], 0.0).astype(BF16)
            it["o"] = _dot(it["qh"], it["state"].astype(BF16)) + _dot(att, it["vh"])

        def stage_commit(n):
            bi, _, h = items[n]
            it = live.pop(n)
            o_scr[it["rows"], it["vs"]] = it["o"]
            so_ref[bi, h] = it["state"] * it["decay"] + it["update"]

        stages = (stage_scores, stage_outputs, stage_commit)
        return [functools.partial(stage, step - depth)
                for step in range(len(items) + len(stages) - 1)
                for depth, stage in enumerate(stages) if 0 <= step - depth < len(items)]

    def output_units(rs):
        st = {"parts": []}

        def norm_head(h):
            def run():
                oh = o_scr[rs, h * GLA_DVH:(h + 1) * GLA_DVH]
                st["parts"].append(oh * _rms_scale(oh) * gn_ref[...])
            return run

        def gate():
            r = r_scr[rs, :]
            gated = jnp.concatenate(st.pop("parts"), axis=-1) * (r * (1.0 / (1.0 + jnp.exp(-r))))
            st["gated"] = gated.astype(BF16)
            st["mix"] = []

        def project(c0):
            def run():
                st["mix"].append(_dot(st["gated"], wo_ref[:, c0:c0 + MXU_COLS]))
            return run

        def residual_norm():
            mix = jnp.concatenate(st.pop("mix"), axis=-1)
            x1_ref[rs, :] = _layer_norm(DEEPNORM_ALPHA * x_ref[rs, :] + mix, lng_ref[...], lnb_ref[...])

        return ([norm_head(h) for h in range(GLA_HEADS)] + [gate]
                + [project(c0) for c0 in range(0, D_MODEL, MXU_COLS)] + [residual_norm])

    first, second = slice(0, half), slice(half, 2 * half)
    proj_a, st_a = projection_units(first)
    proj_b, st_b = projection_units(second)
    schedule = (proj_a
                + _interleave(proj_b, decay_units(first, st_a))
                + _interleave(recurrence_units(0), decay_units(second, st_b))
                + _interleave(recurrence_units(half), output_units(first))
                + output_units(second))
    for unit in schedule:
        unit()


def _gla_layer(x2d, state_in, wm, wgl, wg, bg, gn, wo, lng, lnb, *, batch, seq, chunk):
    n = x2d.shape[0]
    tile = min(GLA_TILE, seq)
    nt = seq // tile
    nb = max(1, min(GLA_SAMPLE_ROWS // seq, batch)) if tile == seq else 1
    m = nb * tile
    row = pl.BlockSpec((m, D_MODEL), lambda g, t: (g * nt + t, 0))
    st = pl.BlockSpec((nb, GLA_HEADS, GLA_DKH, GLA_DVH), lambda g, t: (g, 0, 0, 0))
    return pl.pallas_call(
        functools.partial(_gla_layer_kernel, nb=nb, tile=tile, chunk=chunk),
        grid=(batch // nb, nt),
        in_specs=[row, st, _resident(wm.shape), _resident(wgl.shape), _resident(wg.shape), _resident(bg.shape),
                  _resident(gn.shape), _resident(wo.shape), _resident(lng.shape), _resident(lnb.shape)],
        out_specs=[row, st],
        out_shape=[jax.ShapeDtypeStruct((n, D_MODEL), F32),
                   jax.ShapeDtypeStruct((batch, GLA_HEADS, GLA_DKH, GLA_DVH), F32)],
        scratch_shapes=[pltpu.VMEM((m, GLA_DK), BF16), pltpu.VMEM((m, GLA_DK), BF16),
                        pltpu.VMEM((m, GLA_DK), F32), pltpu.VMEM((m, GLA_DK), F32),
                        pltpu.VMEM((m, GLA_DV), BF16), pltpu.VMEM((m, GLA_DV), F32),
                        pltpu.VMEM((m, GLA_DV), F32)],
        compiler_params=_params(("parallel", "arbitrary")),
        name="gla_layer",
    )(x2d, state_in, wm, wgl, wg, bg, gn, wo, lng, lnb)


def _mlp(xb, wup_ref, wdn_ref):
    acc = None
    for c in range(D_FF // FF_CHUNK):
        cs = slice(c * FF_CHUNK, (c + 1) * FF_CHUNK)
        hid = jnp.maximum(_dot(xb, wup_ref[:, cs]), 0.0)
        part = _dot((hid * hid).astype(BF16), wdn_ref[cs, :])
        acc = part if acc is None else acc + part
    return acc


def _rope(x, cos, sin):
    lane = lax.broadcasted_iota(jnp.int32, (x.shape[0], LANES), 1) & (DIFF_HD - 1)
    first = lane < ROPE_HALF
    second = (lane >= ROPE_HALF) & (lane < ROPE_DIMS)
    parts = []
    for j in range(x.shape[1] // LANES):
        xj = x[:, j * LANES:(j + 1) * LANES]
        ahead = pltpu.roll(xj, LANES - ROPE_HALF, axis=1)
        behind = pltpu.roll(xj, ROPE_HALF, axis=1)
        parts.append(jnp.where(first, xj * cos - ahead * sin,
                               jnp.where(second, xj * cos + behind * sin, xj)))
    return jnp.concatenate(parts, axis=-1)


def _rope_feature_major(xt, cos, sin):
    pieces = []
    for m in range(xt.shape[0] // DIFF_HD):
        base = m * DIFF_HD
        a = xt[base:base + ROPE_HALF]
        b = xt[base + ROPE_HALF:base + ROPE_DIMS]
        pieces += [a * cos - b * sin, b * cos + a * sin, xt[base + ROPE_DIMS:base + DIFF_HD]]
    return jnp.concatenate(pieces, axis=0)


def _mlp_kv_kernel(x1_ref, wup_ref, wdn_ref, lng_ref, lnb_ref, wk_ref, wv_ref, wq_ref, cos_ref, sin_ref,
                   kcos_ref, ksin_ref, x2_ref, k_ref, v_ref, q_ref, *, feature_major_k):
    x1 = x1_ref[...]
    y = _mlp(x1.astype(BF16), wup_ref, wdn_ref)
    x2 = _layer_norm(DEEPNORM_ALPHA * x1 + y, lng_ref[...], lnb_ref[...])
    x2_ref[...] = x2
    x2b = x2.astype(BF16)
    if feature_major_k:
        k_ref[0] = _rope_feature_major(_dot_nt(wk_ref[...], x2b), kcos_ref[...], ksin_ref[...])
    else:
        k_ref[...] = _rope(_dot(x2b, wk_ref[...]), kcos_ref[...], ksin_ref[...])
    v_ref[...] = _dot(x2b, wv_ref[...])
    q = _rope(_dot(x2b, wq_ref[...]), cos_ref[...], sin_ref[...])
    q_ref[...] = (q * (DIFF_HD ** -0.5 * LOG2E)).astype(BF16)


def _mlp_kv(x1, wup, wdn, lng, lnb, wk, wv, wq, tables, *, batch, seq, feature_major_k):
    n = x1.shape[0]
    tm = min(TOKEN_TILE, n)
    cos, sin, cos_t, sin_t = tables
    if tm > seq:
        cos = jnp.tile(cos, (tm // seq, 1))
        sin = jnp.tile(sin, (tm // seq, 1))
    npos = max(seq // tm, 1)
    row = lambda w: pl.BlockSpec((tm, w), lambda i: (i, 0))
    pos = pl.BlockSpec((tm, LANES), lambda i: (i % npos, 0))
    if feature_major_k:
        kcos, ksin = cos_t, sin_t
        kpos = pl.BlockSpec((ROPE_HALF, tm), lambda i: (0, i % npos))
        k_spec = pl.BlockSpec((1, D_MODEL, tm), lambda i: (i // npos, 0, i % npos))
        k_shape = jax.ShapeDtypeStruct((batch, D_MODEL, seq), F32)
    else:
        kcos, ksin, kpos = cos, sin, pos
        k_spec = row(D_MODEL)
        k_shape = jax.ShapeDtypeStruct((n, D_MODEL), F32)
    return pl.pallas_call(
        functools.partial(_mlp_kv_kernel, feature_major_k=feature_major_k),
        grid=(n // tm,),
        in_specs=[row(D_MODEL), _resident(wup.shape), _resident(wdn.shape), _resident(lng.shape),
                  _resident(lnb.shape), _resident(wk.shape), _resident(wv.shape), _resident(wq.shape),
                  pos, pos, kpos, kpos],
        out_specs=[row(D_MODEL), k_spec, row(D_MODEL), row(D_MODEL)],
        out_shape=[jax.ShapeDtypeStruct((n, D_MODEL), F32), k_shape,
                   jax.ShapeDtypeStruct((n, D_MODEL), F32), jax.ShapeDtypeStruct((n, D_MODEL), BF16)],
        compiler_params=_params(("parallel",)),
        name="mlp_kv",
    )(x1, wup, wdn, lng, lnb, wk, wv, wq, cos, sin, kcos, ksin)


def _out_mlp_kernel(o_ref, x2_ref, wo_ref, lng1_ref, lnb1_ref, wup_ref, wdn_ref, lng2_ref, lnb2_ref, y_ref):
    mix = _dot(o_ref[...], wo_ref[...])
    x3 = _layer_norm(DEEPNORM_ALPHA * x2_ref[...] + mix, lng1_ref[...], lnb1_ref[...])
    y = _mlp(x3.astype(BF16), wup_ref, wdn_ref)
    y_ref[...] = _layer_norm(DEEPNORM_ALPHA * x3 + y, lng2_ref[...], lnb2_ref[...])


def _out_mlp(o, x2, wo, lng1, lnb1, wup, wdn, lng2, lnb2):
    n = x2.shape[0]
    tm = min(TOKEN_TILE, n)
    row = lambda w: pl.BlockSpec((tm, w), lambda i: (i, 0))
    return pl.pallas_call(
        _out_mlp_kernel,
        grid=(n // tm,),
        in_specs=[row(D_MODEL), row(D_MODEL), _resident(wo.shape), _resident(lng1.shape),
                  _resident(lnb1.shape), _resident(wup.shape), _resident(wdn.shape),
                  _resident(lng2.shape), _resident(lnb2.shape)],
        out_specs=row(D_MODEL),
        out_shape=jax.ShapeDtypeStruct((n, D_MODEL), F32),
        compiler_params=_params(("parallel",)),
        name="out_mlp",
    )(o, x2, wo, lng1, lnb1, wup, wdn, lng2, lnb2)


def _lambda_full(lam):
    t1 = jnp.sum(lam[0:1, :] * lam[1:2, :], axis=-1, keepdims=True)
    t2 = jnp.sum(lam[2:3, :] * lam[3:4, :], axis=-1, keepdims=True)
    return jnp.exp(t1) - jnp.exp(t2) + LAMBDA_INIT


def _stack_maps(q):
    qf = q.astype(F32)
    lane = lax.broadcasted_iota(jnp.int32, qf.shape, 1)
    return jnp.concatenate([jnp.where(lane < DIFF_HD, qf, 0.0), jnp.where(lane >= DIFF_HD, qf, 0.0)],
                           axis=0).astype(BF16)


def _head_norm(acc, gn):
    return (acc * _rms_scale(acc) * gn * (1.0 - LAMBDA_INIT)).astype(BF16)


def _lane_fold(x, op):
    out = x[:, 0:LANES]
    for j in range(1, x.shape[1] // LANES):
        out = op(out, x[:, j * LANES:(j + 1) * LANES])
    return out


def _attn_kernel(lam_ref, gn_ref, q_ref, kt_ref, v_ref, o_ref, kb, vb, s_scr, *, tq, nq):
    kb[...] = kt_ref[0].astype(BF16)
    vb[...] = v_ref[0].astype(BF16)
    lam = _lambda_full(lam_ref[...])
    gn = gn_ref[...]
    chunk_shift = CHUNK.bit_length() - 1
    rc = (lax.broadcasted_iota(jnp.int32, (2 * tq, tq), 0) & (tq - 1)) >> chunk_shift
    cc = lax.broadcasted_iota(jnp.int32, (2 * tq, tq), 1) >> chunk_shift
    visible = cc <= rc

    def score_units(i):
        st = {}
        slot = i % 2

        def start():
            st["qq"] = _stack_maps(q_ref[0, i * tq:(i + 1) * tq, :])

        def block(j):
            def run():
                s = _dot(st["qq"], kb[:, j * tq:(j + 1) * tq])
                if j == i:
                    s = jnp.where(visible, s, -jnp.inf)
                s_scr[slot, j] = s
                f = _lane_fold(s, jnp.maximum)
                st["m"] = f if j == 0 else jnp.maximum(st["m"], f)
            return run

        def finish():
            st["m"] = jnp.max(st["m"], axis=-1, keepdims=True)

        return [start] + [block(j) for j in range(i + 1)] + [finish], st

    def softmax_units(i, st):
        slot = i % 2

        def exp_block(j):
            def run():
                e = jnp.exp2(s_scr[slot, j] - st["m"])
                s_scr[slot, j] = e
                f = _lane_fold(e, jnp.add)
                st["l"] = f if j == 0 else st["l"] + f
            return run

        def weights():
            l = jnp.sum(st["l"], axis=-1, keepdims=True)
            st["inv_l0"] = 1.0 / l[0:tq]
            st["rho"] = lam * l[0:tq] / l[tq:]

        def pv_block(j):
            def run():
                e = s_scr[slot, j]
                a = (e[0:tq] - e[tq:] * st["rho"]).astype(BF16)
                p = _dot(a, vb[j * tq:(j + 1) * tq, :])
                st["acc"] = p if j == 0 else st["acc"] + p
            return run

        def store():
            o_ref[0, i * tq:(i + 1) * tq, :] = _head_norm(st["acc"] * st["inv_l0"], gn)

        return ([exp_block(j) for j in range(i + 1)] + [weights]
                + [pv_block(j) for j in range(i + 1)] + [store])

    units, st = score_units(0)
    for unit in units:
        unit()
    for i in range(nq):
        ahead, st_next = score_units(i + 1) if i + 1 < nq else ([], None)
        for unit in _interleave(softmax_units(i, st), ahead):
            unit()
        st = st_next


def _diff_attn_prompt(q, kt, v, lam, gn):
    batch, seq, _ = q.shape
    tq = min(ATT_TILE, seq)
    nq = seq // tq
    rows = pl.BlockSpec((1, seq, HEAD_W), lambda b, h: (b, 0, h))
    cols = pl.BlockSpec((1, HEAD_W, seq), lambda b, h: (b, h, 0))
    return pl.pallas_call(
        functools.partial(_attn_kernel, tq=tq, nq=nq),
        grid=(batch, DIFF_HEADS),
        in_specs=[_resident(lam.shape), _resident(gn.shape), rows, cols, rows],
        out_specs=rows,
        out_shape=jax.ShapeDtypeStruct((batch, seq, D_MODEL), BF16),
        scratch_shapes=[pltpu.VMEM((HEAD_W, seq), BF16), pltpu.VMEM((seq, HEAD_W), BF16),
                        pltpu.VMEM((2, nq, 2 * tq, tq), F32)],
        compiler_params=_params(("parallel", "parallel")),
        name="diff_attn_prompt",
    )(lam, gn, q, kt, v)


def _attn_sample_kernel(lam_ref, gn_ref, q_ref, ckt_ref, cv_ref, kn_ref, vn_ref, o_ref):
    lam = _lambda_full(lam_ref[...])
    gn = gn_ref[...]
    t = q_ref.shape[1]
    past = ckt_ref.shape[2]
    live = {}

    def stage_scores(h):
        hs = slice(h * HEAD_W, (h + 1) * HEAD_W)
        qq = _stack_maps(q_ref[0, :, hs])
        live[h] = dict(hs=hs, sp=_dot(qq, ckt_ref[0, hs, :].astype(BF16)),
                       sn=_dot_nt(qq, kn_ref[0, :, hs].astype(BF16)))

    def stage_softmax(h):
        it = live[h]
        sp, sn = it.pop("sp"), it.pop("sn")
        m = jnp.maximum(jnp.max(sp, axis=-1, keepdims=True), jnp.max(sn, axis=-1, keepdims=True))
        ep = jnp.exp2(sp - m)
        en = jnp.exp2(sn - m)
        l = jnp.sum(ep, axis=-1, keepdims=True) + jnp.sum(en, axis=-1, keepdims=True)
        rho = lam * l[0:t] / l[t:]
        it["ap"] = (ep[0:t] - ep[t:] * rho).astype(BF16)
        it["an"] = (en[0:t] - en[t:] * rho).astype(BF16)
        it["inv_l0"] = 1.0 / l[0:t]

    def stage_values(h):
        it = live.pop(h)
        cv = cv_ref[0, pl.ds(h, past, stride=DIFF_HEADS), :]
        acc = _dot(it["ap"], cv.astype(BF16)) + _dot(it["an"], vn_ref[0, :, it["hs"]].astype(BF16))
        o_ref[0, :, it["hs"]] = _head_norm(acc * it["inv_l0"], gn)

    stages = (stage_scores, stage_softmax, stage_values)
    for step in range(DIFF_HEADS + len(stages) - 1):
        for depth, stage in enumerate(stages):
            if 0 <= step - depth < DIFF_HEADS:
                stage(step - depth)


def _diff_attn_sample(q, k_new, v_new, cache_kt, cache_v, lam, gn):
    batch, t, _ = q.shape
    past = cache_v.shape[1]
    new = pl.BlockSpec((1, t, D_MODEL), lambda b: (b, 0, 0))
    return pl.pallas_call(
        _attn_sample_kernel,
        grid=(batch,),
        in_specs=[_resident(lam.shape), _resident(gn.shape), new,
                  pl.BlockSpec((1, D_MODEL, past), lambda b: (b, 0, 0)),
                  pl.BlockSpec((1, past * DIFF_HEADS, HEAD_W), lambda b: (b, 0, 0)), new, new],
        out_specs=new,
        out_shape=jax.ShapeDtypeStruct((batch, t, D_MODEL), BF16),
        compiler_params=_params(("parallel",)),
        name="diff_attn_sample",
    )(lam, gn, q, cache_kt, cache_v.reshape(batch, past * DIFF_HEADS, HEAD_W), k_new, v_new)


def _rope_tables(pos):
    inv = 1.0 / (ROPE_THETA ** (jnp.arange(0, ROPE_DIMS, 2, dtype=F32) / ROPE_DIMS))
    ang = pos.astype(F32)[:, None] * inv[None, :]
    cos, sin = jnp.cos(ang), jnp.sin(ang)
    idx = jnp.arange(LANES) % ROPE_HALF
    return cos[:, idx], sin[:, idx], cos.T, sin.T


def _prepare_weights(w_in_a, w_gate_up_a, b_gate_a, g_norm_a, w_o_a, w_kv, w_q_b, lam_b, g_norm_b, w_o_b,
                     w_up, w_down, ln_g, ln_b):
    n_main = 2 * GLA_DK + 2 * GLA_DV
    w_in = w_in_a[0]
    row = lambda a: a.reshape(1, -1)
    wk = w_kv[:, :D_MODEL].astype(BF16)
    return dict(
        wm=w_in[:, :n_main].astype(BF16),
        wgl=jnp.pad(w_in[:, n_main:], ((0, 0), (0, LANES - GATE_RANK))).astype(BF16),
        wg=jnp.pad(w_gate_up_a[0], ((0, LANES - GATE_RANK), (0, 0))).astype(BF16),
        bg=row(b_gate_a[0]),
        gn_a=row(g_norm_a[0]),
        wo_a=w_o_a[0].astype(BF16),
        wk=wk,
        wk_t=wk.T,
        wv=w_kv[:, D_MODEL:].astype(BF16),
        wq=w_q_b[0].astype(BF16),
        lam=lam_b[0],
        gn_b=row(g_norm_b[0]),
        wo_b=w_o_b[0].astype(BF16),
        wup=[w_up[i].astype(BF16) for i in range(DEPTH)],
        wdn=[w_down[i].astype(BF16) for i in range(DEPTH)],
        lng=[[row(ln_g[i, j]) for j in range(2)] for i in range(DEPTH)],
        lnb=[[row(ln_b[i, j]) for j in range(2)] for i in range(DEPTH)],
    )


def _trunk(x, pos, state_in, cache_k, cache_v, w):
    batch, seq, _ = x.shape
    n = batch * seq
    prompt = cache_k is None
    x2d = x.reshape(n, D_MODEL)
    x1, state = _gla_layer(x2d, state_in, w["wm"], w["wgl"], w["wg"], w["bg"], w["gn_a"], w["wo_a"],
                           w["lng"][0][0], w["lnb"][0][0], batch=batch, seq=seq, chunk=min(CHUNK, seq))
    x2, k_sh, v_sh, q_b = _mlp_kv(x1, w["wup"][0], w["wdn"][0], w["lng"][0][1], w["lnb"][0][1],
                                  w["wk_t"] if prompt else w["wk"], w["wv"], w["wq"], _rope_tables(pos),
                                  batch=batch, seq=seq, feature_major_k=prompt)
    v3 = v_sh.reshape(batch, seq, D_MODEL)
    q3 = q_b.reshape(batch, seq, D_MODEL)
    if prompt:
        o = _diff_attn_prompt(q3, k_sh, v3, w["lam"], w["gn_b"])
        k_out = k_sh.reshape(batch, DIFF_HEADS, 2, DIFF_HD, seq).transpose(0, 4, 1, 2, 3)
    else:
        past = cache_k.shape[1]
        cache_kt = cache_k.transpose(0, 2, 3, 4, 1).reshape(batch, D_MODEL, past)
        o = _diff_attn_sample(q3, k_sh.reshape(batch, seq, D_MODEL), v3, cache_kt, cache_v, w["lam"], w["gn_b"])
        k_out = k_sh.reshape(batch, seq, DIFF_HEADS, 2, DIFF_HD)
    y = _out_mlp(o.reshape(n, D_MODEL), x2, w["wo_b"], w["lng"][1][0], w["lnb"][1][0],
                 w["wup"][1], w["wdn"][1], w["lng"][1][1], w["lnb"][1][1])
    return (y.reshape(batch, seq, D_MODEL), state[None], k_out, v3.reshape(batch, seq, DIFF_HEADS, HEAD_W))


def kernel(x_prompt, x_sample, state_gla, cache_k, cache_v, w_in_a, w_gate_up_a, b_gate_a, g_norm_a, w_o_a, w_kv, w_q_b, lam_b, g_norm_b, w_o_b, w_up, w_down, ln_g, ln_b):
    w = _prepare_weights(w_in_a, w_gate_up_a, b_gate_a, g_norm_a, w_o_a, w_kv, w_q_b, lam_b, g_norm_b,
                         w_o_b, w_up, w_down, ln_g, ln_b)
    batch, seq, _ = x_prompt.shape
    past = cache_k.shape[1]
    zero_state = jnp.zeros((batch, GLA_HEADS, GLA_DKH, GLA_DVH), F32)
    y_p, s_p, k_p, v_p = _trunk(x_prompt, jnp.arange(seq), zero_state, None, None, w)
    y_s, s_s, k_s, v_s = _trunk(x_sample, past + jnp.arange(x_sample.shape[1]), state_gla[0],
                                cache_k, cache_v, w)
    return (y_p, y_s, s_p, k_p, v_p, s_s, k_s, v_s)
```

```python
import functools
import math

import jax
import jax.numpy as jnp
from jax import lax
from jax.experimental import pallas as pl
from jax.experimental.pallas import tpu as pltpu

F32 = jnp.float32
BF16 = jnp.bfloat16

D_MODEL = 1024
DEPTH = 2
CHUNK = 64
GLA_HEADS = 4
GLA_DK = D_MODEL // 2
GLA_DV = D_MODEL
GLA_DKH = GLA_DK // GLA_HEADS
GLA_DVH = GLA_DV // GLA_HEADS
GATE_RANK = 16
GATE_TAU = 16.0
DIFF_HEADS = 8
DIFF_HD = D_MODEL // (2 * DIFF_HEADS)
HEAD_W = 2 * DIFF_HD
ROPE_DIMS = DIFF_HD // 4
ROPE_HALF = ROPE_DIMS // 2
ROPE_THETA = 500000.0
D_FF = 4 * D_MODEL
DEEPNORM_ALPHA = (2 * DEPTH) ** 0.25
LN_EPS = 1e-5
RMS_EPS = 1e-5
LAMBDA_INIT = 0.8 - 0.6 * math.exp(-0.3 * 1)
LOG2E = math.log2(math.e)

LANES = 128
SUBLANES = 8
MXU_COLS = 256
VMEM_LIMIT = 56 * 1024 * 1024
TOKEN_TILE = 512
TOKEN_PARTS = 2
NORM_ROWS = 64
GLA_TILE = 512
GLA_PARTS = 2
GLA_SAMPLE_ROWS = 128
ATT_TILE = 256
ATT_HEADS = 1
FF_CHUNK = 1024

assert ROPE_HALF == SUBLANES


def _dot(a, b):
    return jnp.dot(a, b, preferred_element_type=F32)


def _dot_nt(a, b):
    return lax.dot_general(a, b, (((1,), (1,)), ((), ())), preferred_element_type=F32)


def _layer_norm(z, g, b):
    mu = jnp.mean(z, axis=-1, keepdims=True)
    zc = z - mu
    var = jnp.mean(zc * zc, axis=-1, keepdims=True)
    return zc * lax.rsqrt(var + LN_EPS) * g + b


def _rms_scale(x):
    return lax.rsqrt(jnp.mean(x * x, axis=-1, keepdims=True) + RMS_EPS)


def _resident(shape):
    zeros = (0,) * len(shape)
    return pl.BlockSpec(shape, lambda *_: zeros, pipeline_mode=pl.Buffered(1))


def _interleave(*lists):
    keyed = [(j / len(units), i, j, unit)
             for i, units in enumerate(lists) for j, unit in enumerate(units)]
    return [unit for _, _, _, unit in sorted(keyed, key=lambda t: t[:3])]


def _params(semantics):
    return pltpu.CompilerParams(dimension_semantics=semantics, vmem_limit_bytes=VMEM_LIMIT)


def _gla_layer_kernel(x_ref, si_ref, wm_ref, wgl_ref, wg_ref, bg_ref, gn_ref, wo_ref, lng_ref, lnb_ref,
                      x1_ref, so_ref, qf_scr, kf_scr, k_scr, b_scr, v_scr, r_scr, o_scr, *, nb, tile, chunk):
    @pl.when(pl.program_id(1) == 0)
    def _():
        so_ref[...] = si_ref[...]

    ci = lax.broadcasted_iota(jnp.int32, (chunk, chunk), 0)
    cj = lax.broadcasted_iota(jnp.int32, (chunk, chunk), 1)
    causal = cj <= ci
    part = nb * tile // GLA_PARTS
    row_in_chunk = lax.broadcasted_iota(jnp.int32, (part, MXU_COLS), 0) & (chunk - 1)

    def projection_units(rs):
        st = {}

        def cast():
            st["xb"] = x_ref[rs, :].astype(BF16)

        def gate_down():
            st["gl"] = _dot(st["xb"], wgl_ref[...]).astype(BF16)

        def gate_up(c0):
            def run():
                logit = _dot(st["gl"], wg_ref[:, c0:c0 + MXU_COLS]) + bg_ref[:, c0:c0 + MXU_COLS]
                log_sig = jnp.minimum(logit, 0.0) - jnp.log(1.0 + jnp.exp(-jnp.abs(logit)))
                st.setdefault("b", []).append(log_sig * (1.0 / GATE_TAU))
            return run

        def proj(c0):
            def run():
                y = _dot(st["xb"], wm_ref[:, c0:c0 + MXU_COLS])
                if c0 < GLA_DK:
                    st.setdefault("q", []).append(y * (GLA_DKH ** -0.5))
                elif c0 < 2 * GLA_DK:
                    st.setdefault("k", []).append(y)
                elif c0 < 2 * GLA_DK + GLA_DV:
                    v_scr[rs, c0 - 2 * GLA_DK:c0 - 2 * GLA_DK + MXU_COLS] = y.astype(BF16)
                else:
                    c = c0 - 2 * GLA_DK - GLA_DV
                    r_scr[rs, c:c + MXU_COLS] = y
            return run

        n_main = 2 * GLA_DK + 2 * GLA_DV
        matmuls = [proj(c0) for c0 in range(0, n_main, MXU_COLS)]
        gates = [gate_up(c0) for c0 in range(0, GLA_DK, MXU_COLS)]
        return [cast, gate_down] + _interleave(matmuls, gates), st

    def decay_units(rs, st):
        def scan_step(g, s):
            def run():
                b = st["b"][g]
                st["b"][g] = b + jnp.where(row_in_chunk >= s, pltpu.roll(b, s, axis=0), 0.0)
            return run

        def fold(g):
            def run():
                cols = slice(g * MXU_COLS, (g + 1) * MXU_COLS)
                q, k, b = st["q"][g], st["k"][g], st["b"][g]
                qf_scr[rs, cols] = (q * jnp.exp(b)).astype(BF16)
                kf_scr[rs, cols] = (k * jnp.exp(-b)).astype(BF16)
                k_scr[rs, cols] = k
                b_scr[rs, cols] = b
            return run

        shifts = [1 << p for p in range(chunk.bit_length() - 1)]
        groups = range(GLA_DK // MXU_COLS)
        return [scan_step(g, s) for s in shifts for g in groups] + [fold(g) for g in groups]

    def recurrence_units(row0):
        seqs = range(row0 // tile, (row0 + part - 1) // tile + 1)
        items = [(bi, r0, h) for bi in seqs
                 for r0 in range(max(row0, bi * tile), min(row0 + part, (bi + 1) * tile), chunk)
                 for h in range(GLA_HEADS)]
        chunk_vals = {}
        live = {}

        def chunk_prelude(r0):
            rows = slice(r0, r0 + chunk)
            b_c = b_scr[rows, :]
            b_last = b_c[chunk - 1:chunk, :]
            k_tail = (k_scr[rows, :] * jnp.exp(b_last - b_c)).astype(BF16)
            decay_rows = jnp.broadcast_to(jnp.exp(b_last), (SUBLANES, GLA_DK))
            return rows, k_tail, decay_rows

        def stage_scores(n):
            _, r0, h = items[n]
            if h == 0:
                chunk_vals[r0] = chunk_prelude(r0)
            rows, k_tail, decay_rows = chunk_vals[r0]
            ks = slice(h * GLA_DKH, (h + 1) * GLA_DKH)
            vs = slice(h * GLA_DVH, (h + 1) * GLA_DVH)
            qh = qf_scr[rows, ks]
            vh = v_scr[rows, vs]
            live[n] = dict(
                rows=rows, vs=vs, qh=qh, vh=vh,
                scores=_dot_nt(qh, kf_scr[rows, ks]),
                update=lax.dot_general(k_tail[:, ks], vh, (((0,), (0,)), ((), ())),
                                       preferred_element_type=F32),
                decay=decay_rows[:, ks].T[:, 0:1])

        def stage_outputs(n):
            bi, _, h = items[n]
            it = live[n]
            it["state"] = so_ref[bi, h]
            att = jnp.where(causal, it["scores"], 0.0).astype(BF16)
            it["o"] = _dot(it["qh"], it["state"].astype(BF16)) + _dot(att, it["vh"])

        def stage_commit(n):
            bi, _, h = items[n]
            it = live.pop(n)
            o_scr[it["rows"], it["vs"]] = it["o"]
            so_ref[bi, h] = it["state"] * it["decay"] + it["update"]

        stages = (stage_scores, stage_outputs, stage_commit)
        return [functools.partial(stage, step - depth)
                for step in range(len(items) + len(stages) - 1)
                for depth, stage in enumerate(stages) if 0 <= step - depth < len(items)]

    def output_units(rs):
        st = {"parts": []}

        def norm_head(h):
            def run():
                oh = o_scr[rs, h * GLA_DVH:(h + 1) * GLA_DVH]
                st["parts"].append(oh * _rms_scale(oh) * gn_ref[...])
            return run

        def gate():
            r = r_scr[rs, :]
            gated = jnp.concatenate(st.pop("parts"), axis=-1) * (r * (1.0 / (1.0 + jnp.exp(-r))))
            st["gated"] = gated.astype(BF16)
            st["mix"] = []

        def project(c0):
            def run():
                st["mix"].append(_dot(st["gated"], wo_ref[:, c0:c0 + MXU_COLS]))
            return run

        def residual_norm():
            mix = jnp.concatenate(st.pop("mix"), axis=-1)
            x1_ref[rs, :] = _layer_norm(DEEPNORM_ALPHA * x_ref[rs, :] + mix, lng_ref[...], lnb_ref[...])

        return ([norm_head(h) for h in range(GLA_HEADS)] + [gate]
                + [project(c0) for c0 in range(0, D_MODEL, MXU_COLS)] + [residual_norm])

    states = {}

    def projection_phase(i):
        units, states[i] = projection_units(slice(i * part, (i + 1) * part))
        return units

    def decay_phase(i):
        return decay_units(slice(i * part, (i + 1) * part), states.pop(i))

    def recurrence_phase(i):
        return recurrence_units(i * part)

    def output_phase(i):
        return output_units(slice(i * part, (i + 1) * part))

    _run_pipeline(GLA_PARTS, (projection_phase, decay_phase, recurrence_phase, output_phase))


def _gla_layer(x2d, state_in, wm, wgl, wg, bg, gn, wo, lng, lnb, *, batch, seq, chunk):
    n = x2d.shape[0]
    tile = min(GLA_TILE, seq)
    nt = seq // tile
    nb = max(1, min(GLA_SAMPLE_ROWS // seq, batch)) if tile == seq else 1
    m = nb * tile
    row = pl.BlockSpec((m, D_MODEL), lambda g, t: (g * nt + t, 0))
    st = pl.BlockSpec((nb, GLA_HEADS, GLA_DKH, GLA_DVH), lambda g, t: (g, 0, 0, 0))
    return pl.pallas_call(
        functools.partial(_gla_layer_kernel, nb=nb, tile=tile, chunk=chunk),
        grid=(batch // nb, nt),
        in_specs=[row, st, _resident(wm.shape), _resident(wgl.shape), _resident(wg.shape), _resident(bg.shape),
                  _resident(gn.shape), _resident(wo.shape), _resident(lng.shape), _resident(lnb.shape)],
        out_specs=[row, st],
        out_shape=[jax.ShapeDtypeStruct((n, D_MODEL), F32),
                   jax.ShapeDtypeStruct((batch, GLA_HEADS, GLA_DKH, GLA_DVH), F32)],
        scratch_shapes=[pltpu.VMEM((m, GLA_DK), BF16), pltpu.VMEM((m, GLA_DK), BF16),
                        pltpu.VMEM((m, GLA_DK), F32), pltpu.VMEM((m, GLA_DK), F32),
                        pltpu.VMEM((m, GLA_DV), BF16), pltpu.VMEM((m, GLA_DV), F32),
                        pltpu.VMEM((m, GLA_DV), F32)],
        compiler_params=_params(("parallel", "arbitrary")),
        name="gla_layer",
    )(x2d, state_in, wm, wgl, wg, bg, gn, wo, lng, lnb)


def _mlp_units(st, wup_ref, wdn_ref):
    col_groups = range(FF_CHUNK // MXU_COLS)

    def up(c, g):
        def run():
            c0 = c * FF_CHUNK + g * MXU_COLS
            hid = jnp.maximum(_dot(st["xb"], wup_ref[:, c0:c0 + MXU_COLS]), 0.0)
            st.setdefault("hid", {})[g] = (hid * hid).astype(BF16)
        return run

    def down(c, j):
        def run():
            if j == 0:
                st["hid_chunk"] = jnp.concatenate([st["hid"].pop(g) for g in col_groups], axis=-1)
            p = _dot(st["hid_chunk"], wdn_ref[c * FF_CHUNK:(c + 1) * FF_CHUNK, j * MXU_COLS:(j + 1) * MXU_COLS])
            y = st.setdefault("y", {})
            y[j] = p if c == 0 else y[j] + p
        return run

    units = []
    for c in range(D_FF // FF_CHUNK):
        units += [up(c, g) for g in col_groups] + [down(c, j) for j in range(D_MODEL // MXU_COLS)]
    return units


def _norm_units(rs, src, sink):
    step = min(NORM_ROWS, rs.stop - rs.start)

    def unit(r0):
        def run():
            rows = slice(r0, r0 + step)
            sink(rows, src(rows))
        return run
    return [unit(r0) for r0 in range(rs.start, rs.stop, step)]


def _run_pipeline(parts, phases):
    for s in range(parts + len(phases) - 1):
        live = [phase(s - p) for p, phase in enumerate(phases) if 0 <= s - p < parts]
        for unit in _interleave(*live):
            unit()


def _rope(x, cos, sin):
    lane = lax.broadcasted_iota(jnp.int32, (x.shape[0], LANES), 1) & (DIFF_HD - 1)
    first = lane < ROPE_HALF
    second = (lane >= ROPE_HALF) & (lane < ROPE_DIMS)
    parts = []
    for j in range(x.shape[1] // LANES):
        xj = x[:, j * LANES:(j + 1) * LANES]
        ahead = pltpu.roll(xj, LANES - ROPE_HALF, axis=1)
        behind = pltpu.roll(xj, ROPE_HALF, axis=1)
        parts.append(jnp.where(first, xj * cos - ahead * sin,
                               jnp.where(second, xj * cos + behind * sin, xj)))
    return jnp.concatenate(parts, axis=-1)


def _rope_feature_major(xt, cos, sin):
    pieces = []
    for m in range(xt.shape[0] // DIFF_HD):
        base = m * DIFF_HD
        a = xt[base:base + ROPE_HALF]
        b = xt[base + ROPE_HALF:base + ROPE_DIMS]
        pieces += [a * cos - b * sin, b * cos + a * sin, xt[base + ROPE_DIMS:base + DIFF_HD]]
    return jnp.concatenate(pieces, axis=0)


def _mlp_kv_kernel(x1_ref, wup_ref, wdn_ref, lng_ref, lnb_ref, wk_ref, wv_ref, wq_ref, cos_ref, sin_ref,
                   kcos_ref, ksin_ref, x2_ref, k_ref, v_ref, q_ref, *rest, feature_major_k, parts):
    if feature_major_k:
        kb_ref, vb_ref, x2b_scr = rest
    else:
        (x2b_scr,) = rest
    part = x1_ref.shape[0] // parts
    states = {}
    col_starts = range(0, D_MODEL, MXU_COLS)

    def mlp_phase(i):
        rs = slice(i * part, (i + 1) * part)
        st = states[i] = {}

        def start():
            st["xb"] = x1_ref[rs, :].astype(BF16)

        return [start] + _mlp_units(st, wup_ref, wdn_ref)

    def norm_phase(i):
        rs = slice(i * part, (i + 1) * part)
        st = states.pop(i)

        def gather():
            st["y"] = jnp.concatenate([st["y"][j] for j in range(len(col_starts))], axis=-1)

        def src(rows):
            local = slice(rows.start - rs.start, rows.stop - rs.start)
            return _layer_norm(DEEPNORM_ALPHA * x1_ref[rows, :] + st["y"][local], lng_ref[...], lnb_ref[...])

        def sink(rows, x2):
            x2_ref[rows, :] = x2
            x2b_scr[rows, :] = x2.astype(BF16)

        return [gather] + _norm_units(rs, src, sink)

    def proj_phase(i):
        rs = slice(i * part, (i + 1) * part)
        st = {}

        def start():
            st["xb"] = x2b_scr[rs, :]

        def proj_k(c0):
            def run():
                cs = slice(c0, c0 + MXU_COLS)
                if feature_major_k:
                    kt = _rope_feature_major(_dot_nt(wk_ref[cs, :], st["xb"]), kcos_ref[:, rs], ksin_ref[:, rs])
                    k_ref[0, cs, rs] = kt
                    kb_ref[0, cs, rs] = kt.astype(BF16)
                else:
                    k_ref[rs, cs] = _rope(_dot(st["xb"], wk_ref[:, cs]), kcos_ref[rs, :], ksin_ref[rs, :])
            return run

        def proj_v(c0):
            def run():
                v = _dot(st["xb"], wv_ref[:, c0:c0 + MXU_COLS])
                v_ref[rs, c0:c0 + MXU_COLS] = v
                if feature_major_k:
                    vb_ref[rs, c0:c0 + MXU_COLS] = v.astype(BF16)
            return run

        def proj_q(c0):
            def run():
                q = _rope(_dot(st["xb"], wq_ref[:, c0:c0 + MXU_COLS]), cos_ref[rs, :], sin_ref[rs, :])
                q_ref[rs, c0:c0 + MXU_COLS] = (q * (DIFF_HD ** -0.5 * LOG2E)).astype(BF16)
            return run

        return [start] + [f(c0) for c0 in col_starts for f in (proj_k, proj_v, proj_q)]

    _run_pipeline(parts, (mlp_phase, norm_phase, proj_phase))


def _mlp_kv(x1, wup, wdn, lng, lnb, wk, wv, wq, tables, *, batch, seq, feature_major_k):
    n = x1.shape[0]
    tm = min(TOKEN_TILE, n)
    cos, sin, cos_t, sin_t = tables
    if tm > seq:
        cos = jnp.tile(cos, (tm // seq, 1))
        sin = jnp.tile(sin, (tm // seq, 1))
    npos = max(seq // tm, 1)
    row = lambda w: pl.BlockSpec((tm, w), lambda i: (i, 0))
    pos = pl.BlockSpec((tm, LANES), lambda i: (i % npos, 0))
    if feature_major_k:
        kcos, ksin = cos_t, sin_t
        kpos = pl.BlockSpec((ROPE_HALF, tm), lambda i: (0, i % npos))
        k_spec = pl.BlockSpec((1, D_MODEL, tm), lambda i: (i // npos, 0, i % npos))
        k_shape = jax.ShapeDtypeStruct((batch, D_MODEL, seq), F32)
    else:
        kcos, ksin, kpos = cos, sin, pos
        k_spec = row(D_MODEL)
        k_shape = jax.ShapeDtypeStruct((n, D_MODEL), F32)
    out_specs = [row(D_MODEL), k_spec, row(D_MODEL), row(D_MODEL)]
    out_shape = [jax.ShapeDtypeStruct((n, D_MODEL), F32), k_shape,
                 jax.ShapeDtypeStruct((n, D_MODEL), F32), jax.ShapeDtypeStruct((n, D_MODEL), BF16)]
    if feature_major_k:
        out_specs += [k_spec, row(D_MODEL)]
        out_shape += [jax.ShapeDtypeStruct(k_shape.shape, BF16), jax.ShapeDtypeStruct((n, D_MODEL), BF16)]
    return pl.pallas_call(
        functools.partial(_mlp_kv_kernel, feature_major_k=feature_major_k, parts=TOKEN_PARTS),
        grid=(n // tm,),
        in_specs=[row(D_MODEL), _resident(wup.shape), _resident(wdn.shape), _resident(lng.shape),
                  _resident(lnb.shape), _resident(wk.shape), _resident(wv.shape), _resident(wq.shape),
                  pos, pos, kpos, kpos],
        out_specs=out_specs,
        out_shape=out_shape,
        scratch_shapes=[pltpu.VMEM((tm, D_MODEL), BF16)],
        compiler_params=_params(("parallel",)),
        name="mlp_kv",
    )(x1, wup, wdn, lng, lnb, wk, wv, wq, cos, sin, kcos, ksin)


def _out_mlp_kernel(o_ref, x2_ref, wo_ref, lng1_ref, lnb1_ref, wup_ref, wdn_ref, lng2_ref, lnb2_ref, y_ref,
                    x3_scr, x3b_scr, *, parts):
    part = o_ref.shape[0] // parts
    states = {}
    col_starts = range(0, D_MODEL, MXU_COLS)

    def mix_phase(i):
        rs = slice(i * part, (i + 1) * part)
        st = {"mix": []}

        def project(c0):
            def run():
                st["mix"].append(_dot(o_ref[rs, :], wo_ref[:, c0:c0 + MXU_COLS]))
            return run

        def gather():
            st["mix"] = jnp.concatenate(st["mix"], axis=-1)

        def src(rows):
            local = slice(rows.start - rs.start, rows.stop - rs.start)
            return _layer_norm(DEEPNORM_ALPHA * x2_ref[rows, :] + st["mix"][local], lng1_ref[...], lnb1_ref[...])

        def sink(rows, x3):
            x3_scr[rows, :] = x3
            x3b_scr[rows, :] = x3.astype(BF16)

        return [project(c0) for c0 in col_starts] + [gather] + _norm_units(rs, src, sink)

    def mlp_phase(i):
        rs = slice(i * part, (i + 1) * part)
        st = states[i] = {}

        def start():
            st["xb"] = x3b_scr[rs, :]

        return [start] + _mlp_units(st, wup_ref, wdn_ref)

    def final_phase(i):
        rs = slice(i * part, (i + 1) * part)
        st = states.pop(i)

        def gather():
            st["y"] = jnp.concatenate([st["y"][j] for j in range(len(col_starts))], axis=-1)

        def src(rows):
            local = slice(rows.start - rs.start, rows.stop - rs.start)
            return _layer_norm(DEEPNORM_ALPHA * x3_scr[rows, :] + st["y"][local], lng2_ref[...], lnb2_ref[...])

        def sink(rows, y):
            y_ref[rows, :] = y

        return [gather] + _norm_units(rs, src, sink)

    _run_pipeline(parts, (mix_phase, mlp_phase, final_phase))


def _out_mlp(o, x2, wo, lng1, lnb1, wup, wdn, lng2, lnb2):
    n = x2.shape[0]
    tm = min(TOKEN_TILE, n)
    row = lambda w: pl.BlockSpec((tm, w), lambda i: (i, 0))
    return pl.pallas_call(
        functools.partial(_out_mlp_kernel, parts=TOKEN_PARTS),
        grid=(n // tm,),
        in_specs=[row(D_MODEL), row(D_MODEL), _resident(wo.shape), _resident(lng1.shape),
                  _resident(lnb1.shape), _resident(wup.shape), _resident(wdn.shape),
                  _resident(lng2.shape), _resident(lnb2.shape)],
        out_specs=row(D_MODEL),
        out_shape=jax.ShapeDtypeStruct((n, D_MODEL), F32),
        scratch_shapes=[pltpu.VMEM((tm, D_MODEL), F32), pltpu.VMEM((tm, D_MODEL), BF16)],
        compiler_params=_params(("parallel",)),
        name="out_mlp",
    )(o, x2, wo, lng1, lnb1, wup, wdn, lng2, lnb2)


def _lambda_full(lam):
    t1 = jnp.sum(lam[0:1, :] * lam[1:2, :], axis=-1, keepdims=True)
    t2 = jnp.sum(lam[2:3, :] * lam[3:4, :], axis=-1, keepdims=True)
    return jnp.exp(t1) - jnp.exp(t2) + LAMBDA_INIT


def _stack_maps(q):
    qf = q.astype(F32)
    lane = lax.broadcasted_iota(jnp.int32, qf.shape, 1)
    return jnp.concatenate([jnp.where(lane < DIFF_HD, qf, 0.0), jnp.where(lane >= DIFF_HD, qf, 0.0)],
                           axis=0).astype(BF16)


def _head_norm(acc, gn):
    return (acc * _rms_scale(acc) * gn * (1.0 - LAMBDA_INIT)).astype(BF16)


def _lane_fold(x, op):
    out = x[:, 0:LANES]
    for j in range(1, x.shape[1] // LANES):
        out = op(out, x[:, j * LANES:(j + 1) * LANES])
    return out


def _attn_kernel(lam_ref, gn_ref, q_ref, kt_ref, v_ref, o_ref, s_scr, *, tq, nq, heads):
    lam = _lambda_full(lam_ref[...])
    gn = gn_ref[...]
    chunk_shift = CHUNK.bit_length() - 1
    rc = (lax.broadcasted_iota(jnp.int32, (2 * tq, tq), 0) & (tq - 1)) >> chunk_shift
    cc = lax.broadcasted_iota(jnp.int32, (2 * tq, tq), 1) >> chunk_shift
    visible = cc <= rc

    order = [(g, i) for g in range(heads)
             for i in (range(nq) if g % 2 == 0 else range(nq - 1, -1, -1))]
    states = {}

    def score_phase(pos):
        g, i = order[pos]
        hs = slice(g * HEAD_W, (g + 1) * HEAD_W)
        st = states[pos] = {}
        slot = pos % 2

        def start():
            st["qq"] = _stack_maps(q_ref[0, i * tq:(i + 1) * tq, hs])

        def block(j):
            def run():
                s = _dot(st["qq"], kt_ref[0, hs, j * tq:(j + 1) * tq])
                if j == i:
                    s = jnp.where(visible, s, -jnp.inf)
                s_scr[slot, j] = s
                f = _lane_fold(s, jnp.maximum)
                st["m"] = f if j == 0 else jnp.maximum(st["m"], f)
            return run

        def finish():
            st["m"] = jnp.max(st["m"], axis=-1, keepdims=True)

        return [start] + [block(j) for j in range(i + 1)] + [finish]

    def softmax_phase(pos):
        g, i = order[pos]
        hs = slice(g * HEAD_W, (g + 1) * HEAD_W)
        st = states.pop(pos)
        slot = pos % 2

        def exp_block(j):
            def run():
                e = jnp.exp2(s_scr[slot, j] - st["m"])
                s_scr[slot, j] = e
                f = _lane_fold(e, jnp.add)
                st["l"] = f if j == 0 else st["l"] + f
            return run

        def weights():
            l = jnp.sum(st["l"], axis=-1, keepdims=True)
            st["inv_l0"] = 1.0 / l[0:tq]
            st["rho"] = lam * l[0:tq] / l[tq:]

        def pv_block(j):
            def run():
                e = s_scr[slot, j]
                a = (e[0:tq] - e[tq:] * st["rho"]).astype(BF16)
                p = _dot(a, v_ref[0, j * tq:(j + 1) * tq, hs])
                st["acc"] = p if j == 0 else st["acc"] + p
            return run

        def store():
            o_ref[0, i * tq:(i + 1) * tq, hs] = _head_norm(st["acc"] * st["inv_l0"], gn)

        return ([exp_block(j) for j in range(i + 1)] + [weights]
                + [pv_block(j) for j in range(i + 1)] + [store])

    _run_pipeline(len(order), (score_phase, softmax_phase))


def _diff_attn_prompt(q, kt, v, lam, gn):
    batch, seq, _ = q.shape
    tq = min(ATT_TILE, seq)
    nq = seq // tq
    width = ATT_HEADS * HEAD_W
    rows = pl.BlockSpec((1, seq, width), lambda b, h: (b, 0, h))
    cols = pl.BlockSpec((1, width, seq), lambda b, h: (b, h, 0))
    return pl.pallas_call(
        functools.partial(_attn_kernel, tq=tq, nq=nq, heads=ATT_HEADS),
        grid=(batch, DIFF_HEADS // ATT_HEADS),
        in_specs=[_resident(lam.shape), _resident(gn.shape), rows, cols, rows],
        out_specs=rows,
        out_shape=jax.ShapeDtypeStruct((batch, seq, D_MODEL), BF16),
        scratch_shapes=[pltpu.VMEM((2, nq, 2 * tq, tq), F32)],
        compiler_params=_params(("parallel", "parallel")),
        name="diff_attn_prompt",
    )(lam, gn, q, kt, v)


def _attn_sample_kernel(lam_ref, gn_ref, q_ref, ckt_ref, cv_ref, kn_ref, vn_ref, o_ref):
    lam = _lambda_full(lam_ref[...])
    gn = gn_ref[...]
    t = q_ref.shape[1]
    past = ckt_ref.shape[2]
    live = {}

    def stage_scores(h):
        hs = slice(h * HEAD_W, (h + 1) * HEAD_W)
        qq = _stack_maps(q_ref[0, :, hs])
        live[h] = dict(hs=hs, sp=_dot(qq, ckt_ref[0, hs, :].astype(BF16)),
                       sn=_dot_nt(qq, kn_ref[0, :, hs].astype(BF16)))

    def stage_softmax(h):
        it = live[h]
        sp, sn = it.pop("sp"), it.pop("sn")
        m = jnp.maximum(jnp.max(sp, axis=-1, keepdims=True), jnp.max(sn, axis=-1, keepdims=True))
        ep = jnp.exp2(sp - m)
        en = jnp.exp2(sn - m)
        l = jnp.sum(ep, axis=-1, keepdims=True) + jnp.sum(en, axis=-1, keepdims=True)
        rho = lam * l[0:t] / l[t:]
        it["ap"] = (ep[0:t] - ep[t:] * rho).astype(BF16)
        it["an"] = (en[0:t] - en[t:] * rho).astype(BF16)
        it["inv_l0"] = 1.0 / l[0:t]

    def stage_values(h):
        it = live.pop(h)
        cv = cv_ref[0, pl.ds(h, past, stride=DIFF_HEADS), :]
        acc = _dot(it["ap"], cv.astype(BF16)) + _dot(it["an"], vn_ref[0, :, it["hs"]].astype(BF16))
        o_ref[0, :, it["hs"]] = _head_norm(acc * it["inv_l0"], gn)

    stages = (stage_scores, stage_softmax, stage_values)
    for step in range(DIFF_HEADS + len(stages) - 1):
        for depth, stage in enumerate(stages):
            if 0 <= step - depth < DIFF_HEADS:
                stage(step - depth)


def _diff_attn_sample(q, k_new, v_new, cache_kt, cache_v, lam, gn):
    batch, t, _ = q.shape
    past = cache_v.shape[1]
    new = pl.BlockSpec((1, t, D_MODEL), lambda b: (b, 0, 0))
    return pl.pallas_call(
        _attn_sample_kernel,
        grid=(batch,),
        in_specs=[_resident(lam.shape), _resident(gn.shape), new,
                  pl.BlockSpec((1, D_MODEL, past), lambda b: (b, 0, 0)),
                  pl.BlockSpec((1, past * DIFF_HEADS, HEAD_W), lambda b: (b, 0, 0)), new, new],
        out_specs=new,
        out_shape=jax.ShapeDtypeStruct((batch, t, D_MODEL), BF16),
        compiler_params=_params(("parallel",)),
        name="diff_attn_sample",
    )(lam, gn, q, cache_kt, cache_v.reshape(batch, past * DIFF_HEADS, HEAD_W), k_new, v_new)


def _rope_tables(pos):
    inv = 1.0 / (ROPE_THETA ** (jnp.arange(0, ROPE_DIMS, 2, dtype=F32) / ROPE_DIMS))
    ang = pos.astype(F32)[:, None] * inv[None, :]
    cos, sin = jnp.cos(ang), jnp.sin(ang)
    idx = jnp.arange(LANES) % ROPE_HALF
    return cos[:, idx], sin[:, idx], cos.T, sin.T


def _prepare_weights(w_in_a, w_gate_up_a, b_gate_a, g_norm_a, w_o_a, w_kv, w_q_b, lam_b, g_norm_b, w_o_b,
                     w_up, w_down, ln_g, ln_b):
    n_main = 2 * GLA_DK + 2 * GLA_DV
    w_in = w_in_a[0]
    row = lambda a: a.reshape(1, -1)
    wk = w_kv[:, :D_MODEL].astype(BF16)
    return dict(
        wm=w_in[:, :n_main].astype(BF16),
        wgl=jnp.pad(w_in[:, n_main:], ((0, 0), (0, LANES - GATE_RANK))).astype(BF16),
        wg=jnp.pad(w_gate_up_a[0], ((0, LANES - GATE_RANK), (0, 0))).astype(BF16),
        bg=row(b_gate_a[0]),
        gn_a=row(g_norm_a[0]),
        wo_a=w_o_a[0].astype(BF16),
        wk=wk,
        wk_t=wk.T,
        wv=w_kv[:, D_MODEL:].astype(BF16),
        wq=w_q_b[0].astype(BF16),
        lam=lam_b[0],
        gn_b=row(g_norm_b[0]),
        wo_b=w_o_b[0].astype(BF16),
        wup=[w_up[i].astype(BF16) for i in range(DEPTH)],
        wdn=[w_down[i].astype(BF16) for i in range(DEPTH)],
        lng=[[row(ln_g[i, j]) for j in range(2)] for i in range(DEPTH)],
        lnb=[[row(ln_b[i, j]) for j in range(2)] for i in range(DEPTH)],
    )


def _trunk(x, pos, state_in, cache_k, cache_v, w):
    batch, seq, _ = x.shape
    n = batch * seq
    prompt = cache_k is None
    x2d = x.reshape(n, D_MODEL)
    x1, state = _gla_layer(x2d, state_in, w["wm"], w["wgl"], w["wg"], w["bg"], w["gn_a"], w["wo_a"],
                           w["lng"][0][0], w["lnb"][0][0], batch=batch, seq=seq, chunk=min(CHUNK, seq))
    x2, k_sh, v_sh, q_b, *kv_bf16 = _mlp_kv(x1, w["wup"][0], w["wdn"][0], w["lng"][0][1], w["lnb"][0][1],
                                            w["wk_t"] if prompt else w["wk"], w["wv"], w["wq"],
                                            _rope_tables(pos), batch=batch, seq=seq, feature_major_k=prompt)
    v3 = v_sh.reshape(batch, seq, D_MODEL)
    q3 = q_b.reshape(batch, seq, D_MODEL)
    if prompt:
        kt_b, v_b = kv_bf16
        o = _diff_attn_prompt(q3, kt_b, v_b.reshape(batch, seq, D_MODEL), w["lam"], w["gn_b"])
        k_out = k_sh.reshape(batch, DIFF_HEADS, 2, DIFF_HD, seq).transpose(0, 4, 1, 2, 3)
    else:
        past = cache_k.shape[1]
        cache_kt = cache_k.transpose(0, 2, 3, 4, 1).reshape(batch, D_MODEL, past)
        o = _diff_attn_sample(q3, k_sh.reshape(batch, seq, D_MODEL), v3, cache_kt, cache_v, w["lam"], w["gn_b"])
        k_out = k_sh.reshape(batch, seq, DIFF_HEADS, 2, DIFF_HD)
    y = _out_mlp(o.reshape(n, D_MODEL), x2, w["wo_b"], w["lng"][1][0], w["lnb"][1][0],
                 w["wup"][1], w["wdn"][1], w["lng"][1][1], w["lnb"][1][1])
    return (y.reshape(batch, seq, D_MODEL), state[None], k_out, v3.reshape(batch, seq, DIFF_HEADS, HEAD_W))


def kernel(x_prompt, x_sample, state_gla, cache_k, cache_v, w_in_a, w_gate_up_a, b_gate_a, g_norm_a, w_o_a, w_kv, w_q_b, lam_b, g_norm_b, w_o_b, w_up, w_down, ln_g, ln_b):
    w = _prepare_weights(w_in_a, w_gate_up_a, b_gate_a, g_norm_a, w_o_a, w_kv, w_q_b, lam_b, g_norm_b,
                         w_o_b, w_up, w_down, ln_g, ln_b)
    batch, seq, _ = x_prompt.shape
    past = cache_k.shape[1]
    zero_state = jnp.zeros((batch, GLA_HEADS, GLA_DKH, GLA_DVH), F32)
    y_p, s_p, k_p, v_p = _trunk(x_prompt, jnp.arange(seq), zero_state, None, None, w)
    y_s, s_s, k_s, v_s = _trunk(x_sample, past + jnp.arange(x_sample.shape[1]), state_gla[0],
                                cache_k, cache_v, w)
    return (y_p, y_s, s_p, k_p, v_p, s_s, k_s, v_s)
```

```python
import functools
import math

import jax
import jax.numpy as jnp
from jax import lax
from jax.experimental import pallas as pl
from jax.experimental.pallas import tpu as pltpu

F32 = jnp.float32
BF16 = jnp.bfloat16

D_MODEL = 1024
DEPTH = 2
CHUNK = 64
GLA_HEADS = 4
GLA_DK = D_MODEL // 2
GLA_DV = D_MODEL
GLA_DKH = GLA_DK // GLA_HEADS
GLA_DVH = GLA_DV // GLA_HEADS
GATE_RANK = 16
GATE_TAU = 16.0
DIFF_HEADS = 8
DIFF_HD = D_MODEL // (2 * DIFF_HEADS)
HEAD_W = 2 * DIFF_HD
ROPE_DIMS = DIFF_HD // 4
ROPE_HALF = ROPE_DIMS // 2
ROPE_THETA = 500000.0
D_FF = 4 * D_MODEL
DEEPNORM_ALPHA = (2 * DEPTH) ** 0.25
LN_EPS = 1e-5
RMS_EPS = 1e-5
LAMBDA_INIT = 0.8 - 0.6 * math.exp(-0.3 * 1)
LOG2E = math.log2(math.e)

LANES = 128
SUBLANES = 8
MXU_COLS = 256
VMEM_LIMIT = 56 * 1024 * 1024
TOKEN_TILE = 512
TOKEN_PARTS = 2
NORM_ROWS = 64
GLA_TILE = 512
GLA_PARTS = 2
GLA_SAMPLE_ROWS = 128
ATT_TILE = 256
FF_CHUNK = 1024

assert ROPE_HALF == SUBLANES


def _dot(a, b):
    return jnp.dot(a, b, preferred_element_type=F32)


def _dot_nt(a, b):
    return lax.dot_general(a, b, (((1,), (1,)), ((), ())), preferred_element_type=F32)


def _layer_norm(z, g, b):
    mu = jnp.mean(z, axis=-1, keepdims=True)
    zc = z - mu
    var = jnp.mean(zc * zc, axis=-1, keepdims=True)
    return zc * lax.rsqrt(var + LN_EPS) * g + b


def _rms_scale(x):
    return lax.rsqrt(jnp.mean(x * x, axis=-1, keepdims=True) + RMS_EPS)


def _resident(shape):
    zeros = (0,) * len(shape)
    return pl.BlockSpec(shape, lambda *_: zeros, pipeline_mode=pl.Buffered(1))


def _interleave(*lists):
    keyed = [(j / len(units), i, j, unit)
             for i, units in enumerate(lists) for j, unit in enumerate(units)]
    return [unit for _, _, _, unit in sorted(keyed, key=lambda t: t[:3])]


def _params(semantics):
    return pltpu.CompilerParams(dimension_semantics=semantics, vmem_limit_bytes=VMEM_LIMIT)


def _gla_layer_kernel(x_ref, si_ref, wm_ref, wgl_ref, wg_ref, bg_ref, gn_ref, wo_ref, lng_ref, lnb_ref,
                      x1_ref, so_ref, qf_scr, kf_scr, k_scr, b_scr, v_scr, r_scr, o_scr, *, nb, tile, chunk):
    @pl.when(pl.program_id(1) == 0)
    def _():
        so_ref[...] = si_ref[...]

    ci = lax.broadcasted_iota(jnp.int32, (chunk, chunk), 0)
    cj = lax.broadcasted_iota(jnp.int32, (chunk, chunk), 1)
    causal = cj <= ci
    part = nb * tile // GLA_PARTS
    row_in_chunk = lax.broadcasted_iota(jnp.int32, (part, MXU_COLS), 0) & (chunk - 1)

    def projection_units(rs):
        st = {}

        def cast():
            st["xb"] = x_ref[rs, :].astype(BF16)

        def gate_down():
            st["gl"] = _dot(st["xb"], wgl_ref[...]).astype(BF16)

        def gate_up(c0):
            def run():
                logit = _dot(st["gl"], wg_ref[:, c0:c0 + MXU_COLS]) + bg_ref[:, c0:c0 + MXU_COLS]
                log_sig = jnp.minimum(logit, 0.0) - jnp.log(1.0 + jnp.exp(-jnp.abs(logit)))
                st.setdefault("b", []).append(log_sig * (1.0 / GATE_TAU))
            return run

        def proj(c0):
            def run():
                y = _dot(st["xb"], wm_ref[:, c0:c0 + MXU_COLS])
                if c0 < GLA_DK:
                    st.setdefault("q", []).append(y * (GLA_DKH ** -0.5))
                elif c0 < 2 * GLA_DK:
                    st.setdefault("k", []).append(y)
                elif c0 < 2 * GLA_DK + GLA_DV:
                    v_scr[rs, c0 - 2 * GLA_DK:c0 - 2 * GLA_DK + MXU_COLS] = y.astype(BF16)
                else:
                    c = c0 - 2 * GLA_DK - GLA_DV
                    r_scr[rs, c:c + MXU_COLS] = y
            return run

        n_main = 2 * GLA_DK + 2 * GLA_DV
        matmuls = [proj(c0) for c0 in range(0, n_main, MXU_COLS)]
        gates = [gate_up(c0) for c0 in range(0, GLA_DK, MXU_COLS)]
        return [cast, gate_down] + _interleave(matmuls, gates), st

    def decay_units(rs, st):
        def scan_step(g, s):
            def run():
                b = st["b"][g]
                st["b"][g] = b + jnp.where(row_in_chunk >= s, pltpu.roll(b, s, axis=0), 0.0)
            return run

        def fold(g):
            def run():
                cols = slice(g * MXU_COLS, (g + 1) * MXU_COLS)
                q, k, b = st["q"][g], st["k"][g], st["b"][g]
                qf_scr[rs, cols] = (q * jnp.exp(b)).astype(BF16)
                kf_scr[rs, cols] = (k * jnp.exp(-b)).astype(BF16)
                k_scr[rs, cols] = k
                b_scr[rs, cols] = b
            return run

        shifts = [1 << p for p in range(chunk.bit_length() - 1)]
        groups = range(GLA_DK // MXU_COLS)
        return [scan_step(g, s) for s in shifts for g in groups] + [fold(g) for g in groups]

    def recurrence_units(row0):
        seqs = range(row0 // tile, (row0 + part - 1) // tile + 1)
        items = [(bi, r0, h) for bi in seqs
                 for r0 in range(max(row0, bi * tile), min(row0 + part, (bi + 1) * tile), chunk)
                 for h in range(GLA_HEADS)]
        chunk_vals = {}
        live = {}

        def chunk_prelude(r0):
            rows = slice(r0, r0 + chunk)
            b_c = b_scr[rows, :]
            b_last = b_c[chunk - 1:chunk, :]
            k_tail = (k_scr[rows, :] * jnp.exp(b_last - b_c)).astype(BF16)
            decay_rows = jnp.broadcast_to(jnp.exp(b_last), (SUBLANES, GLA_DK))
            return rows, k_tail, decay_rows

        def stage_scores(n):
            _, r0, h = items[n]
            if h == 0:
                chunk_vals[r0] = chunk_prelude(r0)
            rows, k_tail, decay_rows = chunk_vals[r0]
            ks = slice(h * GLA_DKH, (h + 1) * GLA_DKH)
            vs = slice(h * GLA_DVH, (h + 1) * GLA_DVH)
            qh = qf_scr[rows, ks]
            vh = v_scr[rows, vs]
            live[n] = dict(
                rows=rows, vs=vs, qh=qh, vh=vh,
                scores=_dot_nt(qh, kf_scr[rows, ks]),
                update=lax.dot_general(k_tail[:, ks], vh, (((0,), (0,)), ((), ())),
                                       preferred_element_type=F32),
                decay=decay_rows[:, ks].T[:, 0:1])

        def stage_outputs(n):
            bi, _, h = items[n]
            it = live[n]
            it["state"] = so_ref[bi, h]
            att = jnp.where(causal, it["scores"], 0.0).astype(BF16)
            it["o"] = _dot(it["qh"], it["state"].astype(BF16)) + _dot(att, it["vh"])

        def stage_commit(n):
            bi, _, h = items[n]
            it = live.pop(n)
            o_scr[it["rows"], it["vs"]] = it["o"]
            so_ref[bi, h] = it["state"] * it["decay"] + it["update"]

        stages = (stage_scores, stage_outputs, stage_commit)
        return [functools.partial(stage, step - depth)
                for step in range(len(items) + len(stages) - 1)
                for depth, stage in enumerate(stages) if 0 <= step - depth < len(items)]

    def output_units(rs):
        st = {"parts": []}

        def norm_head(h):
            def run():
                oh = o_scr[rs, h * GLA_DVH:(h + 1) * GLA_DVH]
                st["parts"].append(oh * _rms_scale(oh) * gn_ref[...])
            return run

        def gate():
            r = r_scr[rs, :]
            gated = jnp.concatenate(st.pop("parts"), axis=-1) * (r * (1.0 / (1.0 + jnp.exp(-r))))
            st["gated"] = gated.astype(BF16)
            st["mix"] = []

        def project(c0):
            def run():
                st["mix"].append(_dot(st["gated"], wo_ref[:, c0:c0 + MXU_COLS]))
            return run

        def residual_norm():
            mix = jnp.concatenate(st.pop("mix"), axis=-1)
            x1_ref[rs, :] = _layer_norm(DEEPNORM_ALPHA * x_ref[rs, :] + mix, lng_ref[...], lnb_ref[...])

        return ([norm_head(h) for h in range(GLA_HEADS)] + [gate]
                + [project(c0) for c0 in range(0, D_MODEL, MXU_COLS)] + [residual_norm])

    states = {}

    def projection_phase(i):
        units, states[i] = projection_units(slice(i * part, (i + 1) * part))
        return units

    def decay_phase(i):
        return decay_units(slice(i * part, (i + 1) * part), states.pop(i))

    def recurrence_phase(i):
        return recurrence_units(i * part)

    def output_phase(i):
        return output_units(slice(i * part, (i + 1) * part))

    _run_pipeline(GLA_PARTS, (projection_phase, decay_phase, recurrence_phase, output_phase))


def _gla_layer(x2d, state_in, wm, wgl, wg, bg, gn, wo, lng, lnb, *, batch, seq, chunk):
    n = x2d.shape[0]
    tile = min(GLA_TILE, seq)
    nt = seq // tile
    nb = max(1, min(GLA_SAMPLE_ROWS // seq, batch)) if tile == seq else 1
    m = nb * tile
    row = pl.BlockSpec((m, D_MODEL), lambda g, t: (g * nt + t, 0))
    st = pl.BlockSpec((nb, GLA_HEADS, GLA_DKH, GLA_DVH), lambda g, t: (g, 0, 0, 0))
    return pl.pallas_call(
        functools.partial(_gla_layer_kernel, nb=nb, tile=tile, chunk=chunk),
        grid=(batch // nb, nt),
        in_specs=[row, st, _resident(wm.shape), _resident(wgl.shape), _resident(wg.shape), _resident(bg.shape),
                  _resident(gn.shape), _resident(wo.shape), _resident(lng.shape), _resident(lnb.shape)],
        out_specs=[row, st],
        out_shape=[jax.ShapeDtypeStruct((n, D_MODEL), F32),
                   jax.ShapeDtypeStruct((batch, GLA_HEADS, GLA_DKH, GLA_DVH), F32)],
        scratch_shapes=[pltpu.VMEM((m, GLA_DK), BF16), pltpu.VMEM((m, GLA_DK), BF16),
                        pltpu.VMEM((m, GLA_DK), F32), pltpu.VMEM((m, GLA_DK), F32),
                        pltpu.VMEM((m, GLA_DV), BF16), pltpu.VMEM((m, GLA_DV), F32),
                        pltpu.VMEM((m, GLA_DV), F32)],
        compiler_params=_params(("parallel", "arbitrary")),
        name="gla_layer",
    )(x2d, state_in, wm, wgl, wg, bg, gn, wo, lng, lnb)


def _mlp_units(st, wup_ref, wdn_ref):
    col_groups = range(FF_CHUNK // MXU_COLS)

    def up(c, g):
        def run():
            c0 = c * FF_CHUNK + g * MXU_COLS
            hid = jnp.maximum(_dot(st["xb"], wup_ref[:, c0:c0 + MXU_COLS]), 0.0)
            st.setdefault("hid", {})[g] = (hid * hid).astype(BF16)
        return run

    def down(c, j):
        def run():
            if j == 0:
                st["hid_chunk"] = jnp.concatenate([st["hid"].pop(g) for g in col_groups], axis=-1)
            p = _dot(st["hid_chunk"], wdn_ref[c * FF_CHUNK:(c + 1) * FF_CHUNK, j * MXU_COLS:(j + 1) * MXU_COLS])
            y = st.setdefault("y", {})
            y[j] = p if c == 0 else y[j] + p
        return run

    units = []
    for c in range(D_FF // FF_CHUNK):
        units += [up(c, g) for g in col_groups] + [down(c, j) for j in range(D_MODEL // MXU_COLS)]
    return units


def _norm_units(rs, src, sink):
    step = min(NORM_ROWS, rs.stop - rs.start)

    def unit(r0):
        def run():
            rows = slice(r0, r0 + step)
            sink(rows, src(rows))
        return run
    return [unit(r0) for r0 in range(rs.start, rs.stop, step)]


def _pipeline_units(parts, phases):
    units = []
    for s in range(parts + len(phases) - 1):
        live = [phase(s - p) for p, phase in enumerate(phases) if 0 <= s - p < parts]
        units += _interleave(*live)
    return units


def _run_pipeline(parts, phases):
    for unit in _pipeline_units(parts, phases):
        unit()


def _rope(x, cos, sin):
    lane = lax.broadcasted_iota(jnp.int32, (x.shape[0], LANES), 1) & (DIFF_HD - 1)
    first = lane < ROPE_HALF
    second = (lane >= ROPE_HALF) & (lane < ROPE_DIMS)
    parts = []
    for j in range(x.shape[1] // LANES):
        xj = x[:, j * LANES:(j + 1) * LANES]
        ahead = pltpu.roll(xj, LANES - ROPE_HALF, axis=1)
        behind = pltpu.roll(xj, ROPE_HALF, axis=1)
        parts.append(jnp.where(first, xj * cos - ahead * sin,
                               jnp.where(second, xj * cos + behind * sin, xj)))
    return jnp.concatenate(parts, axis=-1)


def _rope_feature_major(xt, cos, sin):
    pieces = []
    for m in range(xt.shape[0] // DIFF_HD):
        base = m * DIFF_HD
        a = xt[base:base + ROPE_HALF]
        b = xt[base + ROPE_HALF:base + ROPE_DIMS]
        pieces += [a * cos - b * sin, b * cos + a * sin, xt[base + ROPE_DIMS:base + DIFF_HD]]
    return jnp.concatenate(pieces, axis=0)


def _mlp_kv_kernel(x1_ref, wup_ref, wdn_ref, lng_ref, lnb_ref, wk_ref, wv_ref, wq_ref, cos_ref, sin_ref,
                   kcos_ref, ksin_ref, x2_ref, k_ref, v_ref, q_ref, *rest, feature_major_k, parts):
    if feature_major_k:
        kb_ref, vb_ref, x2b_scr = rest
    else:
        (x2b_scr,) = rest
    part = x1_ref.shape[0] // parts
    states = {}
    col_starts = range(0, D_MODEL, MXU_COLS)

    def mlp_phase(i):
        rs = slice(i * part, (i + 1) * part)
        st = states[i] = {}

        def start():
            st["xb"] = x1_ref[rs, :].astype(BF16)

        return [start] + _mlp_units(st, wup_ref, wdn_ref)

    def norm_phase(i):
        rs = slice(i * part, (i + 1) * part)
        st = states.pop(i)

        def gather():
            st["y"] = jnp.concatenate([st["y"][j] for j in range(len(col_starts))], axis=-1)

        def src(rows):
            local = slice(rows.start - rs.start, rows.stop - rs.start)
            return _layer_norm(DEEPNORM_ALPHA * x1_ref[rows, :] + st["y"][local], lng_ref[...], lnb_ref[...])

        def sink(rows, x2):
            x2_ref[rows, :] = x2
            x2b_scr[rows, :] = x2.astype(BF16)

        return [gather] + _norm_units(rs, src, sink)

    def proj_phase(i):
        rs = slice(i * part, (i + 1) * part)
        st = {}

        def start():
            st["xb"] = x2b_scr[rs, :]

        def proj_k(c0):
            def run():
                cs = slice(c0, c0 + MXU_COLS)
                if feature_major_k:
                    kt = _rope_feature_major(_dot_nt(wk_ref[cs, :], st["xb"]), kcos_ref[:, rs], ksin_ref[:, rs])
                    k_ref[0, cs, rs] = kt
                    kb_ref[0, cs, rs] = kt.astype(BF16)
                else:
                    k_ref[rs, cs] = _rope(_dot(st["xb"], wk_ref[:, cs]), kcos_ref[rs, :], ksin_ref[rs, :])
            return run

        def proj_v(c0):
            def run():
                v = _dot(st["xb"], wv_ref[:, c0:c0 + MXU_COLS])
                v_ref[rs, c0:c0 + MXU_COLS] = v
                if feature_major_k:
                    vb_ref[rs, c0:c0 + MXU_COLS] = v.astype(BF16)
            return run

        def proj_q(c0):
            def run():
                q = _rope(_dot(st["xb"], wq_ref[:, c0:c0 + MXU_COLS]), cos_ref[rs, :], sin_ref[rs, :])
                q_ref[rs, c0:c0 + MXU_COLS] = (q * (DIFF_HD ** -0.5 * LOG2E)).astype(BF16)
            return run

        return [start] + [f(c0) for c0 in col_starts for f in (proj_k, proj_v, proj_q)]

    _run_pipeline(parts, (mlp_phase, norm_phase, proj_phase))


def _mlp_kv(x1, wup, wdn, lng, lnb, wk, wv, wq, tables, *, batch, seq, feature_major_k):
    n = x1.shape[0]
    tm = min(TOKEN_TILE, n)
    cos, sin, cos_t, sin_t = tables
    if tm > seq:
        cos = jnp.tile(cos, (tm // seq, 1))
        sin = jnp.tile(sin, (tm // seq, 1))
    npos = max(seq // tm, 1)
    row = lambda w: pl.BlockSpec((tm, w), lambda i: (i, 0))
    pos = pl.BlockSpec((tm, LANES), lambda i: (i % npos, 0))
    if feature_major_k:
        kcos, ksin = cos_t, sin_t
        kpos = pl.BlockSpec((ROPE_HALF, tm), lambda i: (0, i % npos))
        k_spec = pl.BlockSpec((1, D_MODEL, tm), lambda i: (i // npos, 0, i % npos))
        k_shape = jax.ShapeDtypeStruct((batch, D_MODEL, seq), F32)
    else:
        kcos, ksin, kpos = cos, sin, pos
        k_spec = row(D_MODEL)
        k_shape = jax.ShapeDtypeStruct((n, D_MODEL), F32)
    out_specs = [row(D_MODEL), k_spec, row(D_MODEL), row(D_MODEL)]
    out_shape = [jax.ShapeDtypeStruct((n, D_MODEL), F32), k_shape,
                 jax.ShapeDtypeStruct((n, D_MODEL), F32), jax.ShapeDtypeStruct((n, D_MODEL), BF16)]
    if feature_major_k:
        out_specs += [k_spec, row(D_MODEL)]
        out_shape += [jax.ShapeDtypeStruct(k_shape.shape, BF16), jax.ShapeDtypeStruct((n, D_MODEL), BF16)]
    return pl.pallas_call(
        functools.partial(_mlp_kv_kernel, feature_major_k=feature_major_k, parts=TOKEN_PARTS),
        grid=(n // tm,),
        in_specs=[row(D_MODEL), _resident(wup.shape), _resident(wdn.shape), _resident(lng.shape),
                  _resident(lnb.shape), _resident(wk.shape), _resident(wv.shape), _resident(wq.shape),
                  pos, pos, kpos, kpos],
        out_specs=out_specs,
        out_shape=out_shape,
        scratch_shapes=[pltpu.VMEM((tm, D_MODEL), BF16)],
        compiler_params=_params(("parallel",)),
        name="mlp_kv",
    )(x1, wup, wdn, lng, lnb, wk, wv, wq, cos, sin, kcos, ksin)


def _out_mlp_kernel(o_ref, x2_ref, wo_ref, lng1_ref, lnb1_ref, wup_ref, wdn_ref, lng2_ref, lnb2_ref, y_ref,
                    x3_scr, x3b_scr, *, parts):
    part = o_ref.shape[0] // parts
    states = {}
    col_starts = range(0, D_MODEL, MXU_COLS)

    def mix_phase(i):
        rs = slice(i * part, (i + 1) * part)
        st = {"mix": []}

        def project(c0):
            def run():
                st["mix"].append(_dot(o_ref[rs, :], wo_ref[:, c0:c0 + MXU_COLS]))
            return run

        def gather():
            st["mix"] = jnp.concatenate(st["mix"], axis=-1)

        def src(rows):
            local = slice(rows.start - rs.start, rows.stop - rs.start)
            return _layer_norm(DEEPNORM_ALPHA * x2_ref[rows, :] + st["mix"][local], lng1_ref[...], lnb1_ref[...])

        def sink(rows, x3):
            x3_scr[rows, :] = x3
            x3b_scr[rows, :] = x3.astype(BF16)

        return [project(c0) for c0 in col_starts] + [gather] + _norm_units(rs, src, sink)

    def mlp_phase(i):
        rs = slice(i * part, (i + 1) * part)
        st = states[i] = {}

        def start():
            st["xb"] = x3b_scr[rs, :]

        return [start] + _mlp_units(st, wup_ref, wdn_ref)

    def final_phase(i):
        rs = slice(i * part, (i + 1) * part)
        st = states.pop(i)

        def gather():
            st["y"] = jnp.concatenate([st["y"][j] for j in range(len(col_starts))], axis=-1)

        def src(rows):
            local = slice(rows.start - rs.start, rows.stop - rs.start)
            return _layer_norm(DEEPNORM_ALPHA * x3_scr[rows, :] + st["y"][local], lng2_ref[...], lnb2_ref[...])

        def sink(rows, y):
            y_ref[rows, :] = y

        return [gather] + _norm_units(rs, src, sink)

    _run_pipeline(parts, (mix_phase, mlp_phase, final_phase))


def _out_mlp(o, x2, wo, lng1, lnb1, wup, wdn, lng2, lnb2):
    n = x2.shape[0]
    tm = min(TOKEN_TILE, n)
    row = lambda w: pl.BlockSpec((tm, w), lambda i: (i, 0))
    return pl.pallas_call(
        functools.partial(_out_mlp_kernel, parts=TOKEN_PARTS),
        grid=(n // tm,),
        in_specs=[row(D_MODEL), row(D_MODEL), _resident(wo.shape), _resident(lng1.shape),
                  _resident(lnb1.shape), _resident(wup.shape), _resident(wdn.shape),
                  _resident(lng2.shape), _resident(lnb2.shape)],
        out_specs=row(D_MODEL),
        out_shape=jax.ShapeDtypeStruct((n, D_MODEL), F32),
        scratch_shapes=[pltpu.VMEM((tm, D_MODEL), F32), pltpu.VMEM((tm, D_MODEL), BF16)],
        compiler_params=_params(("parallel",)),
        name="out_mlp",
    )(o, x2, wo, lng1, lnb1, wup, wdn, lng2, lnb2)


def _lambda_full(lam):
    t1 = jnp.sum(lam[0:1, :] * lam[1:2, :], axis=-1, keepdims=True)
    t2 = jnp.sum(lam[2:3, :] * lam[3:4, :], axis=-1, keepdims=True)
    return jnp.exp(t1) - jnp.exp(t2) + LAMBDA_INIT


def _stack_maps(q):
    qf = q.astype(F32)
    lane = lax.broadcasted_iota(jnp.int32, qf.shape, 1)
    return jnp.concatenate([jnp.where(lane < DIFF_HD, qf, 0.0), jnp.where(lane >= DIFF_HD, qf, 0.0)],
                           axis=0).astype(BF16)


def _head_norm(acc, gn):
    return (acc * _rms_scale(acc) * gn * (1.0 - LAMBDA_INIT)).astype(BF16)


def _lane_fold(x, op):
    out = x[:, 0:LANES]
    for j in range(1, x.shape[1] // LANES):
        out = op(out, x[:, j * LANES:(j + 1) * LANES])
    return out


def _attention_units(q_ref, kt_ref, v_ref, s_scr, lam, gn, store, *, tq, order):
    chunk_shift = CHUNK.bit_length() - 1
    rc = (lax.broadcasted_iota(jnp.int32, (2 * tq, tq), 0) & (tq - 1)) >> chunk_shift
    cc = lax.broadcasted_iota(jnp.int32, (2 * tq, tq), 1) >> chunk_shift
    visible = cc <= rc

    states = {}

    def score_phase(pos):
        i = order[pos]
        st = states[pos] = {}
        slot = pos % 2

        def start():
            st["qq"] = _stack_maps(q_ref[0, i * tq:(i + 1) * tq, :])

        def block(j):
            def run():
                s = _dot(st["qq"], kt_ref[0, :, j * tq:(j + 1) * tq])
                if j == i:
                    s = jnp.where(visible, s, -jnp.inf)
                s_scr[slot, j] = s
                f = _lane_fold(s, jnp.maximum)
                st["m"] = f if j == 0 else jnp.maximum(st["m"], f)
            return run

        def finish():
            st["m"] = jnp.max(st["m"], axis=-1, keepdims=True)

        return [start] + [block(j) for j in range(i + 1)] + [finish]

    def softmax_phase(pos):
        i = order[pos]
        st = states.pop(pos)
        slot = pos % 2

        def exp_block(j):
            def run():
                e = jnp.exp2(s_scr[slot, j] - st["m"])
                s_scr[slot, j] = e
                f = _lane_fold(e, jnp.add)
                st["l"] = f if j == 0 else st["l"] + f
            return run

        def weights():
            l = jnp.sum(st["l"], axis=-1, keepdims=True)
            st["inv_l0"] = 1.0 / l[0:tq]
            st["rho"] = lam * l[0:tq] / l[tq:]

        def pv_block(j):
            def run():
                e = s_scr[slot, j]
                a = (e[0:tq] - e[tq:] * st["rho"]).astype(BF16)
                p = _dot(a, v_ref[0, j * tq:(j + 1) * tq, :])
                st["acc"] = p if j == 0 else st["acc"] + p
            return run

        def finish():
            store(i, _head_norm(st["acc"] * st["inv_l0"], gn))

        return ([exp_block(j) for j in range(i + 1)] + [weights]
                + [pv_block(j) for j in range(i + 1)] + [finish])

    return _pipeline_units(len(order), (score_phase, softmax_phase))


def _attn_mlp_kernel(lam_ref, gn_ref, q_ref, kt_ref, v_ref, x2_ref, wo_ref, lng1_ref, lnb1_ref, wup_ref, wdn_ref,
                     lng2_ref, lnb2_ref, y_ref, s_scr, o_scr, x3_scr, x3b_scr, *, tq, nq, batch):
    b = pl.program_id(0)
    h = pl.program_id(1)
    cur = b % 2
    part = x2_ref.shape[1]

    def attention(order):
        def store(i, o):
            o_scr[cur, h, i * tq:(i + 1) * tq, :] = o

        return _attention_units(q_ref, kt_ref, v_ref, s_scr, _lambda_full(lam_ref[...]), gn_ref[...], store,
                                tq=tq, order=order)

    growing = list(range(nq))
    shrinking = growing[::-1]

    def channel_mixer():
        r0 = pl.multiple_of(h * part, part)
        st = {"mix": []}
        col_starts = range(0, D_MODEL, MXU_COLS)

        def gather_heads():
            st["o"] = jnp.concatenate([o_scr[1 - cur, g, pl.ds(r0, part), :] for g in range(DIFF_HEADS)], axis=-1)

        def project(c0):
            def run():
                st["mix"].append(_dot(st["o"], wo_ref[:, c0:c0 + MXU_COLS]))
            return run

        def gather_mix():
            st["mix"] = jnp.concatenate(st["mix"], axis=-1)

        def norm1_src(rows):
            return _layer_norm(DEEPNORM_ALPHA * x2_ref[0, rows, :] + st["mix"][rows], lng1_ref[...], lnb1_ref[...])

        def norm1_sink(rows, x3):
            x3_scr[rows, :] = x3
            x3b_scr[rows, :] = x3.astype(BF16)

        def load():
            st["xb"] = x3b_scr[...]

        def gather_y():
            st["y"] = jnp.concatenate([st["y"][j] for j in range(len(col_starts))], axis=-1)

        def norm2_src(rows):
            return _layer_norm(DEEPNORM_ALPHA * x3_scr[rows, :] + st["y"][rows], lng2_ref[...], lnb2_ref[...])

        def norm2_sink(rows, y):
            y_ref[0, rows, :] = y

        whole = slice(0, part)
        return ([gather_heads] + [project(c0) for c0 in col_starts] + [gather_mix]
                + _norm_units(whole, norm1_src, norm1_sink) + [load] + _mlp_units(st, wup_ref, wdn_ref)
                + [gather_y] + _norm_units(whole, norm2_src, norm2_sink))

    @pl.when(b == 0)
    def _():
        for unit in attention(growing):
            unit()

    @pl.when((b > 0) & (b < batch))
    def _():
        for unit in _interleave(attention(shrinking), channel_mixer()):
            unit()

    @pl.when(b == batch)
    def _():
        for unit in channel_mixer():
            unit()


def _attn_mlp_prompt(q, kt, v, x2, lam, gn, wo, lng1, lnb1, wup, wdn, lng2, lnb2):
    batch, seq, _ = q.shape
    tq = min(ATT_TILE, seq)
    nq = seq // tq
    part = seq // DIFF_HEADS
    last = batch - 1
    rows = pl.BlockSpec((1, seq, HEAD_W), lambda b, h: (jnp.minimum(b, last), 0, h))
    cols = pl.BlockSpec((1, HEAD_W, seq), lambda b, h: (jnp.minimum(b, last), h, 0))
    lagged = pl.BlockSpec((1, part, D_MODEL),
                          lambda b, h: (jnp.maximum(b - 1, 0), jnp.where(b == 0, 0, h), 0))
    weights = [wo, lng1, lnb1, wup, wdn, lng2, lnb2]
    return pl.pallas_call(
        functools.partial(_attn_mlp_kernel, tq=tq, nq=nq, batch=batch),
        grid=(batch + 1, DIFF_HEADS),
        in_specs=[_resident(lam.shape), _resident(gn.shape), rows, cols, rows, lagged]
                 + [_resident(w.shape) for w in weights],
        out_specs=lagged,
        out_shape=jax.ShapeDtypeStruct((batch, seq, D_MODEL), F32),
        scratch_shapes=[pltpu.VMEM((2, nq, 2 * tq, tq), F32),
                        pltpu.VMEM((2, DIFF_HEADS, seq, HEAD_W), BF16),
                        pltpu.VMEM((part, D_MODEL), F32), pltpu.VMEM((part, D_MODEL), BF16)],
        compiler_params=_params(("arbitrary", "arbitrary")),
        name="attn_mlp",
    )(lam, gn, q, kt, v, x2, *weights)


def _attn_sample_kernel(lam_ref, gn_ref, q_ref, ckt_ref, cv_ref, kn_ref, vn_ref, o_ref):
    lam = _lambda_full(lam_ref[...])
    gn = gn_ref[...]
    t = q_ref.shape[1]
    past = ckt_ref.shape[2]
    live = {}

    def stage_scores(h):
        hs = slice(h * HEAD_W, (h + 1) * HEAD_W)
        qq = _stack_maps(q_ref[0, :, hs])
        live[h] = dict(hs=hs, sp=_dot(qq, ckt_ref[0, hs, :].astype(BF16)),
                       sn=_dot_nt(qq, kn_ref[0, :, hs].astype(BF16)))

    def stage_softmax(h):
        it = live[h]
        sp, sn = it.pop("sp"), it.pop("sn")
        m = jnp.maximum(jnp.max(sp, axis=-1, keepdims=True), jnp.max(sn, axis=-1, keepdims=True))
        ep = jnp.exp2(sp - m)
        en = jnp.exp2(sn - m)
        l = jnp.sum(ep, axis=-1, keepdims=True) + jnp.sum(en, axis=-1, keepdims=True)
        rho = lam * l[0:t] / l[t:]
        it["ap"] = (ep[0:t] - ep[t:] * rho).astype(BF16)
        it["an"] = (en[0:t] - en[t:] * rho).astype(BF16)
        it["inv_l0"] = 1.0 / l[0:t]

    def stage_values(h):
        it = live.pop(h)
        cv = cv_ref[0, pl.ds(h, past, stride=DIFF_HEADS), :]
        acc = _dot(it["ap"], cv.astype(BF16)) + _dot(it["an"], vn_ref[0, :, it["hs"]].astype(BF16))
        o_ref[0, :, it["hs"]] = _head_norm(acc * it["inv_l0"], gn)

    stages = (stage_scores, stage_softmax, stage_values)
    for step in range(DIFF_HEADS + len(stages) - 1):
        for depth, stage in enumerate(stages):
            if 0 <= step - depth < DIFF_HEADS:
                stage(step - depth)


def _diff_attn_sample(q, k_new, v_new, cache_kt, cache_v, lam, gn):
    batch, t, _ = q.shape
    past = cache_v.shape[1]
    new = pl.BlockSpec((1, t, D_MODEL), lambda b: (b, 0, 0))
    return pl.pallas_call(
        _attn_sample_kernel,
        grid=(batch,),
        in_specs=[_resident(lam.shape), _resident(gn.shape), new,
                  pl.BlockSpec((1, D_MODEL, past), lambda b: (b, 0, 0)),
                  pl.BlockSpec((1, past * DIFF_HEADS, HEAD_W), lambda b: (b, 0, 0)), new, new],
        out_specs=new,
        out_shape=jax.ShapeDtypeStruct((batch, t, D_MODEL), BF16),
        compiler_params=_params(("parallel",)),
        name="diff_attn_sample",
    )(lam, gn, q, cache_kt, cache_v.reshape(batch, past * DIFF_HEADS, HEAD_W), k_new, v_new)


def _rope_tables(pos):
    inv = 1.0 / (ROPE_THETA ** (jnp.arange(0, ROPE_DIMS, 2, dtype=F32) / ROPE_DIMS))
    ang = pos.astype(F32)[:, None] * inv[None, :]
    cos, sin = jnp.cos(ang), jnp.sin(ang)
    idx = jnp.arange(LANES) % ROPE_HALF
    return cos[:, idx], sin[:, idx], cos.T, sin.T


def _prepare_weights(w_in_a, w_gate_up_a, b_gate_a, g_norm_a, w_o_a, w_kv, w_q_b, lam_b, g_norm_b, w_o_b,
                     w_up, w_down, ln_g, ln_b):
    n_main = 2 * GLA_DK + 2 * GLA_DV
    w_in = w_in_a[0]
    row = lambda a: a.reshape(1, -1)
    wk = w_kv[:, :D_MODEL].astype(BF16)
    return dict(
        wm=w_in[:, :n_main].astype(BF16),
        wgl=jnp.pad(w_in[:, n_main:], ((0, 0), (0, LANES - GATE_RANK))).astype(BF16),
        wg=jnp.pad(w_gate_up_a[0], ((0, LANES - GATE_RANK), (0, 0))).astype(BF16),
        bg=row(b_gate_a[0]),
        gn_a=row(g_norm_a[0]),
        wo_a=w_o_a[0].astype(BF16),
        wk=wk,
        wk_t=wk.T,
        wv=w_kv[:, D_MODEL:].astype(BF16),
        wq=w_q_b[0].astype(BF16),
        lam=lam_b[0],
        gn_b=row(g_norm_b[0]),
        wo_b=w_o_b[0].astype(BF16),
        wup=[w_up[i].astype(BF16) for i in range(DEPTH)],
        wdn=[w_down[i].astype(BF16) for i in range(DEPTH)],
        lng=[[row(ln_g[i, j]) for j in range(2)] for i in range(DEPTH)],
        lnb=[[row(ln_b[i, j]) for j in range(2)] for i in range(DEPTH)],
    )


def _trunk(x, pos, state_in, cache_k, cache_v, w):
    batch, seq, _ = x.shape
    n = batch * seq
    prompt = cache_k is None
    x2d = x.reshape(n, D_MODEL)
    x1, state = _gla_layer(x2d, state_in, w["wm"], w["wgl"], w["wg"], w["bg"], w["gn_a"], w["wo_a"],
                           w["lng"][0][0], w["lnb"][0][0], batch=batch, seq=seq, chunk=min(CHUNK, seq))
    x2, k_sh, v_sh, q_b, *kv_bf16 = _mlp_kv(x1, w["wup"][0], w["wdn"][0], w["lng"][0][1], w["lnb"][0][1],
                                            w["wk_t"] if prompt else w["wk"], w["wv"], w["wq"],
                                            _rope_tables(pos), batch=batch, seq=seq, feature_major_k=prompt)
    v3 = v_sh.reshape(batch, seq, D_MODEL)
    q3 = q_b.reshape(batch, seq, D_MODEL)
    layer_b = (w["wo_b"], w["lng"][1][0], w["lnb"][1][0], w["wup"][1], w["wdn"][1], w["lng"][1][1], w["lnb"][1][1])
    if prompt:
        kt_b, v_b = kv_bf16
        y = _attn_mlp_prompt(q3, kt_b, v_b.reshape(batch, seq, D_MODEL), x2.reshape(batch, seq, D_MODEL),
                             w["lam"], w["gn_b"], *layer_b)
        k_out = k_sh.reshape(batch, DIFF_HEADS, 2, DIFF_HD, seq).transpose(0, 4, 1, 2, 3)
    else:
        past = cache_k.shape[1]
        cache_kt = cache_k.transpose(0, 2, 3, 4, 1).reshape(batch, D_MODEL, past)
        o = _diff_attn_sample(q3, k_sh.reshape(batch, seq, D_MODEL), v3, cache_kt, cache_v, w["lam"], w["gn_b"])
        k_out = k_sh.reshape(batch, seq, DIFF_HEADS, 2, DIFF_HD)
        y = _out_mlp(o.reshape(n, D_MODEL), x2, *layer_b)
    return (y.reshape(batch, seq, D_MODEL), state[None], k_out, v3.reshape(batch, seq, DIFF_HEADS, HEAD_W))


def kernel(x_prompt, x_sample, state_gla, cache_k, cache_v, w_in_a, w_gate_up_a, b_gate_a, g_norm_a, w_o_a, w_kv, w_q_b, lam_b, g_norm_b, w_o_b, w_up, w_down, ln_g, ln_b):
    w = _prepare_weights(w_in_a, w_gate_up_a, b_gate_a, g_norm_a, w_o_a, w_kv, w_q_b, lam_b, g_norm_b,
                         w_o_b, w_up, w_down, ln_g, ln_b)
    batch, seq, _ = x_prompt.shape
    past = cache_k.shape[1]
    zero_state = jnp.zeros((batch, GLA_HEADS, GLA_DKH, GLA_DVH), F32)
    y_p, s_p, k_p, v_p = _trunk(x_prompt, jnp.arange(seq), zero_state, None, None, w)
    y_s, s_s, k_s, v_s = _trunk(x_sample, past + jnp.arange(x_sample.shape[1]), state_gla[0],
                                cache_k, cache_v, w)
    return (y_p, y_s, s_p, k_p, v_p, s_s, k_s, v_s)
```

```python
import functools
import math

import jax
import jax.numpy as jnp
from jax import lax
from jax.experimental import pallas as pl
from jax.experimental.pallas import tpu as pltpu

F32 = jnp.float32
BF16 = jnp.bfloat16

D_MODEL = 1024
DEPTH = 2
CHUNK = 64
GLA_HEADS = 4
GLA_DK = D_MODEL // 2
GLA_DV = D_MODEL
GLA_DKH = GLA_DK // GLA_HEADS
GLA_DVH = GLA_DV // GLA_HEADS
GATE_RANK = 16
GATE_TAU = 16.0
DIFF_HEADS = 8
DIFF_HD = D_MODEL // (2 * DIFF_HEADS)
HEAD_W = 2 * DIFF_HD
ROPE_DIMS = DIFF_HD // 4
ROPE_HALF = ROPE_DIMS // 2
ROPE_THETA = 500000.0
D_FF = 4 * D_MODEL
DEEPNORM_ALPHA = (2 * DEPTH) ** 0.25
LN_EPS = 1e-5
RMS_EPS = 1e-5
LAMBDA_INIT = 0.8 - 0.6 * math.exp(-0.3 * 1)
LOG2E = math.log2(math.e)

LANES = 128
SUBLANES = 8
MXU_COLS = 256
VMEM_LIMIT = 56 * 1024 * 1024
TOKEN_TILE = 512
TOKEN_PARTS = 2
NORM_ROWS = 64
GLA_TILE = 512
GLA_PARTS = 2
GLA_SAMPLE_ROWS = 128
ATT_TILE = 256
FF_CHUNK = 1024

assert ROPE_HALF == SUBLANES


def _dot(a, b):
    return jnp.dot(a, b, preferred_element_type=F32)


def _dot_nt(a, b):
    return lax.dot_general(a, b, (((1,), (1,)), ((), ())), preferred_element_type=F32)


def _layer_norm(z, g, b):
    mu = jnp.mean(z, axis=-1, keepdims=True)
    zc = z - mu
    var = jnp.mean(zc * zc, axis=-1, keepdims=True)
    return zc * lax.rsqrt(var + LN_EPS) * g + b


def _rms_scale(x):
    return lax.rsqrt(jnp.mean(x * x, axis=-1, keepdims=True) + RMS_EPS)


def _resident(shape):
    zeros = (0,) * len(shape)
    return pl.BlockSpec(shape, lambda *_: zeros, pipeline_mode=pl.Buffered(1))


def _interleave(*lists):
    keyed = [(j / len(units), i, j, unit)
             for i, units in enumerate(lists) for j, unit in enumerate(units)]
    return [unit for _, _, _, unit in sorted(keyed, key=lambda t: t[:3])]


def _params(semantics):
    return pltpu.CompilerParams(dimension_semantics=semantics, vmem_limit_bytes=VMEM_LIMIT)


def _gla_layer_kernel(x_ref, si_ref, wm_ref, wgl_ref, wg_ref, bg_ref, gn_ref, wo_ref, lng_ref, lnb_ref,
                      x1_ref, so_ref, qf_scr, kf_scr, k_scr, b_scr, v_scr, r_scr, o_scr, *, nb, tile, chunk):
    @pl.when(pl.program_id(1) == 0)
    def _():
        so_ref[...] = si_ref[...]

    ci = lax.broadcasted_iota(jnp.int32, (chunk, chunk), 0)
    cj = lax.broadcasted_iota(jnp.int32, (chunk, chunk), 1)
    causal = cj <= ci
    part = nb * tile // GLA_PARTS
    row_in_chunk = lax.broadcasted_iota(jnp.int32, (part, MXU_COLS), 0) & (chunk - 1)

    def projection_units(rs):
        st = {}

        def cast():
            st["xb"] = x_ref[rs, :].astype(BF16)

        def gate_down():
            st["gl"] = _dot(st["xb"], wgl_ref[...]).astype(BF16)

        def gate_up(c0):
            def run():
                logit = _dot(st["gl"], wg_ref[:, c0:c0 + MXU_COLS]) + bg_ref[:, c0:c0 + MXU_COLS]
                log_sig = jnp.minimum(logit, 0.0) - jnp.log(1.0 + jnp.exp(-jnp.abs(logit)))
                st.setdefault("b", []).append(log_sig * (1.0 / GATE_TAU))
            return run

        def proj(c0):
            def run():
                y = _dot(st["xb"], wm_ref[:, c0:c0 + MXU_COLS])
                if c0 < GLA_DK:
                    st.setdefault("q", []).append(y * (GLA_DKH ** -0.5))
                elif c0 < 2 * GLA_DK:
                    st.setdefault("k", []).append(y)
                elif c0 < 2 * GLA_DK + GLA_DV:
                    v_scr[rs, c0 - 2 * GLA_DK:c0 - 2 * GLA_DK + MXU_COLS] = y.astype(BF16)
                else:
                    c = c0 - 2 * GLA_DK - GLA_DV
                    r_scr[rs, c:c + MXU_COLS] = y
            return run

        n_main = 2 * GLA_DK + 2 * GLA_DV
        matmuls = [proj(c0) for c0 in range(0, n_main, MXU_COLS)]
        gates = [gate_up(c0) for c0 in range(0, GLA_DK, MXU_COLS)]
        return [cast, gate_down] + _interleave(matmuls, gates), st

    def decay_units(rs, st):
        def scan_step(g, s):
            def run():
                b = st["b"][g]
                st["b"][g] = b + jnp.where(row_in_chunk >= s, pltpu.roll(b, s, axis=0), 0.0)
            return run

        def fold(g):
            def run():
                cols = slice(g * MXU_COLS, (g + 1) * MXU_COLS)
                q, k, b = st["q"][g], st["k"][g], st["b"][g]
                qf_scr[rs, cols] = (q * jnp.exp(b)).astype(BF16)
                kf_scr[rs, cols] = (k * jnp.exp(-b)).astype(BF16)
                k_scr[rs, cols] = k
                b_scr[rs, cols] = b
            return run

        shifts = [1 << p for p in range(chunk.bit_length() - 1)]
        groups = range(GLA_DK // MXU_COLS)
        return [scan_step(g, s) for s in shifts for g in groups] + [fold(g) for g in groups]

    def recurrence_units(row0):
        seqs = range(row0 // tile, (row0 + part - 1) // tile + 1)
        items = [(bi, r0, h) for bi in seqs
                 for r0 in range(max(row0, bi * tile), min(row0 + part, (bi + 1) * tile), chunk)
                 for h in range(GLA_HEADS)]
        chunk_vals = {}
        live = {}

        def chunk_prelude(r0):
            rows = slice(r0, r0 + chunk)
            b_c = b_scr[rows, :]
            b_last = b_c[chunk - 1:chunk, :]
            k_tail = (k_scr[rows, :] * jnp.exp(b_last - b_c)).astype(BF16)
            decay_rows = jnp.broadcast_to(jnp.exp(b_last), (SUBLANES, GLA_DK))
            return rows, k_tail, decay_rows

        def stage_scores(n):
            _, r0, h = items[n]
            if h == 0:
                chunk_vals[r0] = chunk_prelude(r0)
            rows, k_tail, decay_rows = chunk_vals[r0]
            ks = slice(h * GLA_DKH, (h + 1) * GLA_DKH)
            vs = slice(h * GLA_DVH, (h + 1) * GLA_DVH)
            qh = qf_scr[rows, ks]
            vh = v_scr[rows, vs]
            live[n] = dict(
                rows=rows, vs=vs, qh=qh, vh=vh,
                scores=_dot_nt(qh, kf_scr[rows, ks]),
                update=lax.dot_general(k_tail[:, ks], vh, (((0,), (0,)), ((), ())),
                                       preferred_element_type=F32),
                decay=decay_rows[:, ks].T[:, 0:1])

        def stage_outputs(n):
            bi, _, h = items[n]
            it = live[n]
            it["state"] = so_ref[bi, h]
            att = jnp.where(causal, it["scores"], 0.0).astype(BF16)
            it["o"] = _dot(it["qh"], it["state"].astype(BF16)) + _dot(att, it["vh"])

        def stage_commit(n):
            bi, _, h = items[n]
            it = live.pop(n)
            o_scr[it["rows"], it["vs"]] = it["o"]
            so_ref[bi, h] = it["state"] * it["decay"] + it["update"]

        stages = (stage_scores, stage_outputs, stage_commit)
        return [functools.partial(stage, step - depth)
                for step in range(len(items) + len(stages) - 1)
                for depth, stage in enumerate(stages) if 0 <= step - depth < len(items)]

    def output_units(rs):
        st = {"parts": []}

        def norm_head(h):
            def run():
                oh = o_scr[rs, h * GLA_DVH:(h + 1) * GLA_DVH]
                st["parts"].append(oh * _rms_scale(oh) * gn_ref[...])
            return run

        def gate():
            r = r_scr[rs, :]
            gated = jnp.concatenate(st.pop("parts"), axis=-1) * (r * (1.0 / (1.0 + jnp.exp(-r))))
            st["gated"] = gated.astype(BF16)
            st["mix"] = []

        def project(c0):
            def run():
                st["mix"].append(_dot(st["gated"], wo_ref[:, c0:c0 + MXU_COLS]))
            return run

        def residual_norm():
            mix = jnp.concatenate(st.pop("mix"), axis=-1)
            x1_ref[rs, :] = _layer_norm(DEEPNORM_ALPHA * x_ref[rs, :] + mix, lng_ref[...], lnb_ref[...])

        return ([norm_head(h) for h in range(GLA_HEADS)] + [gate]
                + [project(c0) for c0 in range(0, D_MODEL, MXU_COLS)] + [residual_norm])

    states = {}

    def projection_phase(i):
        units, states[i] = projection_units(slice(i * part, (i + 1) * part))
        return units

    def decay_phase(i):
        return decay_units(slice(i * part, (i + 1) * part), states.pop(i))

    def recurrence_phase(i):
        return recurrence_units(i * part)

    def output_phase(i):
        return output_units(slice(i * part, (i + 1) * part))

    _run_pipeline(GLA_PARTS, (projection_phase, decay_phase, recurrence_phase, output_phase))


def _gla_layer(x2d, state_in, wm, wgl, wg, bg, gn, wo, lng, lnb, *, batch, seq, chunk):
    n = x2d.shape[0]
    tile = min(GLA_TILE, seq)
    nt = seq // tile
    nb = max(1, min(GLA_SAMPLE_ROWS // seq, batch)) if tile == seq else 1
    m = nb * tile
    row = pl.BlockSpec((m, D_MODEL), lambda g, t: (g * nt + t, 0))
    st = pl.BlockSpec((nb, GLA_HEADS, GLA_DKH, GLA_DVH), lambda g, t: (g, 0, 0, 0))
    return pl.pallas_call(
        functools.partial(_gla_layer_kernel, nb=nb, tile=tile, chunk=chunk),
        grid=(batch // nb, nt),
        in_specs=[row, st, _resident(wm.shape), _resident(wgl.shape), _resident(wg.shape), _resident(bg.shape),
                  _resident(gn.shape), _resident(wo.shape), _resident(lng.shape), _resident(lnb.shape)],
        out_specs=[row, st],
        out_shape=[jax.ShapeDtypeStruct((n, D_MODEL), F32),
                   jax.ShapeDtypeStruct((batch, GLA_HEADS, GLA_DKH, GLA_DVH), F32)],
        scratch_shapes=[pltpu.VMEM((m, GLA_DK), BF16), pltpu.VMEM((m, GLA_DK), BF16),
                        pltpu.VMEM((m, GLA_DK), F32), pltpu.VMEM((m, GLA_DK), F32),
                        pltpu.VMEM((m, GLA_DV), BF16), pltpu.VMEM((m, GLA_DV), F32),
                        pltpu.VMEM((m, GLA_DV), F32)],
        compiler_params=_params(("parallel", "arbitrary")),
        name="gla_layer",
    )(x2d, state_in, wm, wgl, wg, bg, gn, wo, lng, lnb)


def _mlp_units(st, wup_ref, wdn_ref):
    col_groups = range(FF_CHUNK // MXU_COLS)

    def up(c, g):
        def run():
            c0 = c * FF_CHUNK + g * MXU_COLS
            hid = jnp.maximum(_dot(st["xb"], wup_ref[:, c0:c0 + MXU_COLS]), 0.0)
            st.setdefault("hid", {})[g] = (hid * hid).astype(BF16)
        return run

    def down(c, j):
        def run():
            if j == 0:
                st["hid_chunk"] = jnp.concatenate([st["hid"].pop(g) for g in col_groups], axis=-1)
            p = _dot(st["hid_chunk"], wdn_ref[c * FF_CHUNK:(c + 1) * FF_CHUNK, j * MXU_COLS:(j + 1) * MXU_COLS])
            y = st.setdefault("y", {})
            y[j] = p if c == 0 else y[j] + p
        return run

    units = []
    for c in range(D_FF // FF_CHUNK):
        units += [up(c, g) for g in col_groups] + [down(c, j) for j in range(D_MODEL // MXU_COLS)]
    return units


def _norm_units(rs, src, sink):
    step = min(NORM_ROWS, rs.stop - rs.start)

    def unit(r0):
        def run():
            rows = slice(r0, r0 + step)
            sink(rows, src(rows))
        return run
    return [unit(r0) for r0 in range(rs.start, rs.stop, step)]


def _pipeline_units(parts, phases):
    units = []
    for s in range(parts + len(phases) - 1):
        live = [phase(s - p) for p, phase in enumerate(phases) if 0 <= s - p < parts]
        units += _interleave(*live)
    return units


def _run_pipeline(parts, phases):
    for unit in _pipeline_units(parts, phases):
        unit()


def _rope(x, cos, sin):
    lane = lax.broadcasted_iota(jnp.int32, (x.shape[0], LANES), 1) & (DIFF_HD - 1)
    first = lane < ROPE_HALF
    second = (lane >= ROPE_HALF) & (lane < ROPE_DIMS)
    parts = []
    for j in range(x.shape[1] // LANES):
        xj = x[:, j * LANES:(j + 1) * LANES]
        ahead = pltpu.roll(xj, LANES - ROPE_HALF, axis=1)
        behind = pltpu.roll(xj, ROPE_HALF, axis=1)
        parts.append(jnp.where(first, xj * cos - ahead * sin,
                               jnp.where(second, xj * cos + behind * sin, xj)))
    return jnp.concatenate(parts, axis=-1)


def _rope_feature_major(xt, cos, sin):
    pieces = []
    for m in range(xt.shape[0] // DIFF_HD):
        base = m * DIFF_HD
        a = xt[base:base + ROPE_HALF]
        b = xt[base + ROPE_HALF:base + ROPE_DIMS]
        pieces += [a * cos - b * sin, b * cos + a * sin, xt[base + ROPE_DIMS:base + DIFF_HD]]
    return jnp.concatenate(pieces, axis=0)


def _mlp_kv_kernel(x1_ref, wup_ref, wdn_ref, lng_ref, lnb_ref, wk_ref, wv_ref, wq_ref, cos_ref, sin_ref,
                   kcos_ref, ksin_ref, x2_ref, k_ref, v_ref, q_ref, *rest, feature_major_k, parts):
    if feature_major_k:
        kb_ref, vb_ref, x2b_scr = rest
    else:
        (x2b_scr,) = rest
    part = x1_ref.shape[0] // parts
    states = {}
    col_starts = range(0, D_MODEL, MXU_COLS)

    def mlp_phase(i):
        rs = slice(i * part, (i + 1) * part)
        st = states[i] = {}

        def start():
            st["xb"] = x1_ref[rs, :].astype(BF16)

        return [start] + _mlp_units(st, wup_ref, wdn_ref)

    def norm_phase(i):
        rs = slice(i * part, (i + 1) * part)
        st = states.pop(i)

        def gather():
            st["y"] = jnp.concatenate([st["y"][j] for j in range(len(col_starts))], axis=-1)

        def src(rows):
            local = slice(rows.start - rs.start, rows.stop - rs.start)
            return _layer_norm(DEEPNORM_ALPHA * x1_ref[rows, :] + st["y"][local], lng_ref[...], lnb_ref[...])

        def sink(rows, x2):
            x2_ref[rows, :] = x2
            x2b_scr[rows, :] = x2.astype(BF16)

        return [gather] + _norm_units(rs, src, sink)

    def proj_phase(i):
        rs = slice(i * part, (i + 1) * part)
        st = {}

        def start():
            st["xb"] = x2b_scr[rs, :]

        def proj_k(c0):
            def run():
                cs = slice(c0, c0 + MXU_COLS)
                if feature_major_k:
                    kt = _rope_feature_major(_dot_nt(wk_ref[cs, :], st["xb"]), kcos_ref[:, rs], ksin_ref[:, rs])
                    k_ref[0, cs, rs] = kt
                    kb_ref[0, cs, rs] = kt.astype(BF16)
                else:
                    k_ref[rs, cs] = _rope(_dot(st["xb"], wk_ref[:, cs]), kcos_ref[rs, :], ksin_ref[rs, :])
            return run

        def proj_v(c0):
            def run():
                v = _dot(st["xb"], wv_ref[:, c0:c0 + MXU_COLS])
                v_ref[rs, c0:c0 + MXU_COLS] = v
                if feature_major_k:
                    vb_ref[rs, c0:c0 + MXU_COLS] = v.astype(BF16)
            return run

        def proj_q(c0):
            def run():
                q = _rope(_dot(st["xb"], wq_ref[:, c0:c0 + MXU_COLS]), cos_ref[rs, :], sin_ref[rs, :])
                q_ref[rs, c0:c0 + MXU_COLS] = (q * (DIFF_HD ** -0.5 * LOG2E)).astype(BF16)
            return run

        return [start] + [f(c0) for c0 in col_starts for f in (proj_k, proj_v, proj_q)]

    _run_pipeline(parts, (mlp_phase, norm_phase, proj_phase))


def _mlp_kv(x1, wup, wdn, lng, lnb, wk, wv, wq, tables, *, batch, seq, feature_major_k):
    n = x1.shape[0]
    tm = min(TOKEN_TILE, n)
    cos, sin, cos_t, sin_t = tables
    if tm > seq:
        cos = jnp.tile(cos, (tm // seq, 1))
        sin = jnp.tile(sin, (tm // seq, 1))
    npos = max(seq // tm, 1)
    row = lambda w: pl.BlockSpec((tm, w), lambda i: (i, 0))
    pos = pl.BlockSpec((tm, LANES), lambda i: (i % npos, 0))
    if feature_major_k:
        kcos, ksin = cos_t, sin_t
        kpos = pl.BlockSpec((ROPE_HALF, tm), lambda i: (0, i % npos))
        k_spec = pl.BlockSpec((1, D_MODEL, tm), lambda i: (i // npos, 0, i % npos))
        k_shape = jax.ShapeDtypeStruct((batch, D_MODEL, seq), F32)
    else:
        kcos, ksin, kpos = cos, sin, pos
        k_spec = row(D_MODEL)
        k_shape = jax.ShapeDtypeStruct((n, D_MODEL), F32)
    out_specs = [row(D_MODEL), k_spec, row(D_MODEL), row(D_MODEL)]
    out_shape = [jax.ShapeDtypeStruct((n, D_MODEL), F32), k_shape,
                 jax.ShapeDtypeStruct((n, D_MODEL), F32), jax.ShapeDtypeStruct((n, D_MODEL), BF16)]
    if feature_major_k:
        out_specs += [k_spec, row(D_MODEL)]
        out_shape += [jax.ShapeDtypeStruct(k_shape.shape, BF16), jax.ShapeDtypeStruct((n, D_MODEL), BF16)]
    return pl.pallas_call(
        functools.partial(_mlp_kv_kernel, feature_major_k=feature_major_k, parts=TOKEN_PARTS),
        grid=(n // tm,),
        in_specs=[row(D_MODEL), _resident(wup.shape), _resident(wdn.shape), _resident(lng.shape),
                  _resident(lnb.shape), _resident(wk.shape), _resident(wv.shape), _resident(wq.shape),
                  pos, pos, kpos, kpos],
        out_specs=out_specs,
        out_shape=out_shape,
        scratch_shapes=[pltpu.VMEM((tm, D_MODEL), BF16)],
        compiler_params=_params(("parallel",)),
        name="mlp_kv",
    )(x1, wup, wdn, lng, lnb, wk, wv, wq, cos, sin, kcos, ksin)


def _out_mlp_kernel(o_ref, x2_ref, wo_ref, lng1_ref, lnb1_ref, wup_ref, wdn_ref, lng2_ref, lnb2_ref, y_ref,
                    x3_scr, x3b_scr, *, parts):
    part = o_ref.shape[0] // parts
    states = {}
    col_starts = range(0, D_MODEL, MXU_COLS)

    def mix_phase(i):
        rs = slice(i * part, (i + 1) * part)
        st = {"mix": []}

        def project(c0):
            def run():
                st["mix"].append(_dot(o_ref[rs, :], wo_ref[:, c0:c0 + MXU_COLS]))
            return run

        def gather():
            st["mix"] = jnp.concatenate(st["mix"], axis=-1)

        def src(rows):
            local = slice(rows.start - rs.start, rows.stop - rs.start)
            return _layer_norm(DEEPNORM_ALPHA * x2_ref[rows, :] + st["mix"][local], lng1_ref[...], lnb1_ref[...])

        def sink(rows, x3):
            x3_scr[rows, :] = x3
            x3b_scr[rows, :] = x3.astype(BF16)

        return [project(c0) for c0 in col_starts] + [gather] + _norm_units(rs, src, sink)

    def mlp_phase(i):
        rs = slice(i * part, (i + 1) * part)
        st = states[i] = {}

        def start():
            st["xb"] = x3b_scr[rs, :]

        return [start] + _mlp_units(st, wup_ref, wdn_ref)

    def final_phase(i):
        rs = slice(i * part, (i + 1) * part)
        st = states.pop(i)

        def gather():
            st["y"] = jnp.concatenate([st["y"][j] for j in range(len(col_starts))], axis=-1)

        def src(rows):
            local = slice(rows.start - rs.start, rows.stop - rs.start)
            return _layer_norm(DEEPNORM_ALPHA * x3_scr[rows, :] + st["y"][local], lng2_ref[...], lnb2_ref[...])

        def sink(rows, y):
            y_ref[rows, :] = y

        return [gather] + _norm_units(rs, src, sink)

    _run_pipeline(parts, (mix_phase, mlp_phase, final_phase))


def _out_mlp(o, x2, wo, lng1, lnb1, wup, wdn, lng2, lnb2):
    n = x2.shape[0]
    tm = min(TOKEN_TILE, n)
    row = lambda w: pl.BlockSpec((tm, w), lambda i: (i, 0))
    return pl.pallas_call(
        functools.partial(_out_mlp_kernel, parts=TOKEN_PARTS),
        grid=(n // tm,),
        in_specs=[row(D_MODEL), row(D_MODEL), _resident(wo.shape), _resident(lng1.shape),
                  _resident(lnb1.shape), _resident(wup.shape), _resident(wdn.shape),
                  _resident(lng2.shape), _resident(lnb2.shape)],
        out_specs=row(D_MODEL),
        out_shape=jax.ShapeDtypeStruct((n, D_MODEL), F32),
        scratch_shapes=[pltpu.VMEM((tm, D_MODEL), F32), pltpu.VMEM((tm, D_MODEL), BF16)],
        compiler_params=_params(("parallel",)),
        name="out_mlp",
    )(o, x2, wo, lng1, lnb1, wup, wdn, lng2, lnb2)


def _lambda_full(lam):
    t1 = jnp.sum(lam[0:1, :] * lam[1:2, :], axis=-1, keepdims=True)
    t2 = jnp.sum(lam[2:3, :] * lam[3:4, :], axis=-1, keepdims=True)
    return jnp.exp(t1) - jnp.exp(t2) + LAMBDA_INIT


def _stack_maps(q):
    qf = q.astype(F32)
    lane = lax.broadcasted_iota(jnp.int32, qf.shape, 1)
    return jnp.concatenate([jnp.where(lane < DIFF_HD, qf, 0.0), jnp.where(lane >= DIFF_HD, qf, 0.0)],
                           axis=0).astype(BF16)


def _head_norm(acc, gn):
    return (acc * _rms_scale(acc) * gn * (1.0 - LAMBDA_INIT)).astype(BF16)


def _lane_fold(x, op):
    out = x[:, 0:LANES]
    for j in range(1, x.shape[1] // LANES):
        out = op(out, x[:, j * LANES:(j + 1) * LANES])
    return out


def _attention_units(q_ref, kt_ref, v_ref, s_scr, lam, gn, store, *, tq, order):
    chunk_shift = CHUNK.bit_length() - 1
    rc = (lax.broadcasted_iota(jnp.int32, (2 * tq, tq), 0) & (tq - 1)) >> chunk_shift
    cc = lax.broadcasted_iota(jnp.int32, (2 * tq, tq), 1) >> chunk_shift
    visible = cc <= rc

    states = {}

    def score_phase(pos):
        i = order[pos]
        st = states[pos] = {}
        slot = pos % 2

        def start():
            st["qq"] = _stack_maps(q_ref[0, i * tq:(i + 1) * tq, :])

        def block(j):
            def run():
                s = _dot(st["qq"], kt_ref[0, :, j * tq:(j + 1) * tq])
                if j == i:
                    s = jnp.where(visible, s, -jnp.inf)
                s_scr[slot, j] = s
                f = _lane_fold(s, jnp.maximum)
                st["m"] = f if j == 0 else jnp.maximum(st["m"], f)
            return run

        def finish():
            st["m"] = jnp.max(st["m"], axis=-1, keepdims=True)

        return [start] + [block(j) for j in range(i + 1)] + [finish]

    def softmax_phase(pos):
        i = order[pos]
        st = states.pop(pos)
        slot = pos % 2

        def exp_block(j):
            def run():
                e = jnp.exp2(s_scr[slot, j] - st["m"])
                s_scr[slot, j] = e
                f = _lane_fold(e, jnp.add)
                st["l"] = f if j == 0 else st["l"] + f
            return run

        def weights():
            l = jnp.sum(st["l"], axis=-1, keepdims=True)
            st["inv_l0"] = 1.0 / l[0:tq]
            st["rho"] = lam * l[0:tq] / l[tq:]

        def pv_block(j):
            def run():
                e = s_scr[slot, j]
                a = (e[0:tq] - e[tq:] * st["rho"]).astype(BF16)
                p = _dot(a, v_ref[0, j * tq:(j + 1) * tq, :])
                st["acc"] = p if j == 0 else st["acc"] + p
            return run

        def finish():
            store(i, _head_norm(st["acc"] * st["inv_l0"], gn))

        return ([exp_block(j) for j in range(i + 1)] + [weights]
                + [pv_block(j) for j in range(i + 1)] + [finish])

    return _pipeline_units(len(order), (score_phase, softmax_phase))


def _attn_mlp_kernel(lam_ref, gn_ref, q_ref, kt_ref, v_ref, x2_ref, wo_ref, lng1_ref, lnb1_ref, wup_ref, wdn_ref,
                     lng2_ref, lnb2_ref, y_ref, s_scr, o_scr, x3_scr, x3b_scr, *, tq, nq, batch):
    b = pl.program_id(0)
    h = pl.program_id(1)
    cur = b % 2
    part = x2_ref.shape[1]

    def attention(order):
        def store(i, o):
            o_scr[cur, h, i * tq:(i + 1) * tq, :] = o

        return _attention_units(q_ref, kt_ref, v_ref, s_scr, _lambda_full(lam_ref[...]), gn_ref[...], store,
                                tq=tq, order=order)

    growing = list(range(nq))
    shrinking = growing[::-1]

    def channel_mixer():
        r0 = pl.multiple_of(h * part, part)
        st = {"mix": []}
        col_starts = range(0, D_MODEL, MXU_COLS)

        def gather_heads():
            st["o"] = jnp.concatenate([o_scr[1 - cur, g, pl.ds(r0, part), :] for g in range(DIFF_HEADS)], axis=-1)

        def project(c0):
            def run():
                st["mix"].append(_dot(st["o"], wo_ref[:, c0:c0 + MXU_COLS]))
            return run

        def gather_mix():
            st["mix"] = jnp.concatenate(st["mix"], axis=-1)

        def norm1_src(rows):
            return _layer_norm(DEEPNORM_ALPHA * x2_ref[0, rows, :] + st["mix"][rows], lng1_ref[...], lnb1_ref[...])

        def norm1_sink(rows, x3):
            x3_scr[rows, :] = x3
            x3b_scr[rows, :] = x3.astype(BF16)

        def load():
            st["xb"] = x3b_scr[...]

        def gather_y():
            st["y"] = jnp.concatenate([st["y"][j] for j in range(len(col_starts))], axis=-1)

        def norm2_src(rows):
            return _layer_norm(DEEPNORM_ALPHA * x3_scr[rows, :] + st["y"][rows], lng2_ref[...], lnb2_ref[...])

        def norm2_sink(rows, y):
            y_ref[0, rows, :] = y

        whole = slice(0, part)
        return ([gather_heads] + [project(c0) for c0 in col_starts] + [gather_mix]
                + _norm_units(whole, norm1_src, norm1_sink) + [load] + _mlp_units(st, wup_ref, wdn_ref)
                + [gather_y] + _norm_units(whole, norm2_src, norm2_sink))

    @pl.when(b == 0)
    def _():
        for unit in attention(growing):
            unit()

    @pl.when((b > 0) & (b < batch))
    def _():
        for unit in _interleave(attention(shrinking), channel_mixer()):
            unit()

    @pl.when(b == batch)
    def _():
        for unit in channel_mixer():
            unit()


def _attn_mlp_prompt(q, kt, v, x2, lam, gn, wo, lng1, lnb1, wup, wdn, lng2, lnb2):
    batch, seq, _ = q.shape
    tq = min(ATT_TILE, seq)
    nq = seq // tq
    part = seq // DIFF_HEADS
    last = batch - 1
    rows = pl.BlockSpec((1, seq, HEAD_W), lambda b, h: (jnp.minimum(b, last), 0, h))
    cols = pl.BlockSpec((1, HEAD_W, seq), lambda b, h: (jnp.minimum(b, last), h, 0))
    lagged = pl.BlockSpec((1, part, D_MODEL),
                          lambda b, h: (jnp.maximum(b - 1, 0), jnp.where(b == 0, 0, h), 0))
    weights = [wo, lng1, lnb1, wup, wdn, lng2, lnb2]
    return pl.pallas_call(
        functools.partial(_attn_mlp_kernel, tq=tq, nq=nq, batch=batch),
        grid=(batch + 1, DIFF_HEADS),
        in_specs=[_resident(lam.shape), _resident(gn.shape), rows, cols, rows, lagged]
                 + [_resident(w.shape) for w in weights],
        out_specs=lagged,
        out_shape=jax.ShapeDtypeStruct((batch, seq, D_MODEL), F32),
        scratch_shapes=[pltpu.VMEM((2, nq, 2 * tq, tq), F32),
                        pltpu.VMEM((2, DIFF_HEADS, seq, HEAD_W), BF16),
                        pltpu.VMEM((part, D_MODEL), F32), pltpu.VMEM((part, D_MODEL), BF16)],
        compiler_params=_params(("arbitrary", "arbitrary")),
        name="attn_mlp",
    )(lam, gn, q, kt, v, x2, *weights)


def _attn_sample_kernel(lam_ref, gn_ref, q_ref, ckt_ref, cv_ref, kn_ref, vn_ref, o_ref):
    lam = _lambda_full(lam_ref[...])
    gn = gn_ref[...]
    t = q_ref.shape[1]
    past = ckt_ref.shape[2]
    live = {}

    def stage_scores(h):
        hs = slice(h * HEAD_W, (h + 1) * HEAD_W)
        qq = _stack_maps(q_ref[0, :, hs])
        live[h] = dict(hs=hs, sp=_dot(qq, ckt_ref[0, hs, :].astype(BF16)),
                       sn=_dot_nt(qq, kn_ref[0, :, hs].astype(BF16)))

    def stage_softmax(h):
        it = live[h]
        sp, sn = it.pop("sp"), it.pop("sn")
        m = jnp.maximum(jnp.max(sp, axis=-1, keepdims=True), jnp.max(sn, axis=-1, keepdims=True))
        ep = jnp.exp2(sp - m)
        en = jnp.exp2(sn - m)
        l = jnp.sum(ep, axis=-1, keepdims=True) + jnp.sum(en, axis=-1, keepdims=True)
        rho = lam * l[0:t] / l[t:]
        it["ap"] = (ep[0:t] - ep[t:] * rho).astype(BF16)
        it["an"] = (en[0:t] - en[t:] * rho).astype(BF16)
        it["inv_l0"] = 1.0 / l[0:t]

    def stage_values(h):
        it = live.pop(h)
        cv = cv_ref[0, pl.ds(h, past, stride=DIFF_HEADS), :]
        acc = _dot(it["ap"], cv.astype(BF16)) + _dot(it["an"], vn_ref[0, :, it["hs"]].astype(BF16))
        o_ref[0, :, it["hs"]] = _head_norm(acc * it["inv_l0"], gn)

    stages = (stage_scores, stage_softmax, stage_values)
    for step in range(DIFF_HEADS + len(stages) - 1):
        for depth, stage in enumerate(stages):
            if 0 <= step - depth < DIFF_HEADS:
                stage(step - depth)


def _diff_attn_sample(q, k_new, v_new, cache_kt, cache_v, lam, gn):
    batch, t, _ = q.shape
    past = cache_v.shape[1]
    new = pl.BlockSpec((1, t, D_MODEL), lambda b: (b, 0, 0))
    return pl.pallas_call(
        _attn_sample_kernel,
        grid=(batch,),
        in_specs=[_resident(lam.shape), _resident(gn.shape), new,
                  pl.BlockSpec((1, D_MODEL, past), lambda b: (b, 0, 0)),
                  pl.BlockSpec((1, past * DIFF_HEADS, HEAD_W), lambda b: (b, 0, 0)), new, new],
        out_specs=new,
        out_shape=jax.ShapeDtypeStruct((batch, t, D_MODEL), BF16),
        compiler_params=_params(("parallel",)),
        name="diff_attn_sample",
    )(lam, gn, q, cache_kt, cache_v.reshape(batch, past * DIFF_HEADS, HEAD_W), k_new, v_new)


def _rope_tables(pos):
    inv = 1.0 / (ROPE_THETA ** (jnp.arange(0, ROPE_DIMS, 2, dtype=F32) / ROPE_DIMS))
    ang = pos.astype(F32)[:, None] * inv[None, :]
    cos, sin = jnp.cos(ang), jnp.sin(ang)
    idx = jnp.arange(LANES) % ROPE_HALF
    return cos[:, idx], sin[:, idx], cos.T, sin.T


def _prepare_weights(w_in_a, w_gate_up_a, b_gate_a, g_norm_a, w_o_a, w_kv, w_q_b, lam_b, g_norm_b, w_o_b,
                     w_up, w_down, ln_g, ln_b):
    n_main = 2 * GLA_DK + 2 * GLA_DV
    w_in = w_in_a[0]
    row = lambda a: a.reshape(1, -1)
    wk = w_kv[:, :D_MODEL].astype(BF16)
    return dict(
        wm=w_in[:, :n_main].astype(BF16),
        wgl=jnp.pad(w_in[:, n_main:], ((0, 0), (0, LANES - GATE_RANK))).astype(BF16),
        wg=jnp.pad(w_gate_up_a[0], ((0, LANES - GATE_RANK), (0, 0))).astype(BF16),
        bg=row(b_gate_a[0]),
        gn_a=row(g_norm_a[0]),
        wo_a=w_o_a[0].astype(BF16),
        wk=wk,
        wk_t=wk.T,
        wv=w_kv[:, D_MODEL:].astype(BF16),
        wq=w_q_b[0].astype(BF16),
        lam=lam_b[0],
        gn_b=row(g_norm_b[0]),
        wo_b=w_o_b[0].astype(BF16),
        wup=[w_up[i].astype(BF16) for i in range(DEPTH)],
        wdn=[w_down[i].astype(BF16) for i in range(DEPTH)],
        lng=[[row(ln_g[i, j]) for j in range(2)] for i in range(DEPTH)],
        lnb=[[row(ln_b[i, j]) for j in range(2)] for i in range(DEPTH)],
    )


def _trunk(x, pos, state_in, cache_k, cache_v, w):
    batch, seq, _ = x.shape
    n = batch * seq
    prompt = cache_k is None
    x2d = x.reshape(n, D_MODEL)
    x1, state = _gla_layer(x2d, state_in, w["wm"], w["wgl"], w["wg"], w["bg"], w["gn_a"], w["wo_a"],
                           w["lng"][0][0], w["lnb"][0][0], batch=batch, seq=seq, chunk=min(CHUNK, seq))
    x2, k_sh, v_sh, q_b, *kv_bf16 = _mlp_kv(x1, w["wup"][0], w["wdn"][0], w["lng"][0][1], w["lnb"][0][1],
                                            w["wk_t"] if prompt else w["wk"], w["wv"], w["wq"],
                                            _rope_tables(pos), batch=batch, seq=seq, feature_major_k=prompt)
    v3 = v_sh.reshape(batch, seq, D_MODEL)
    q3 = q_b.reshape(batch, seq, D_MODEL)
    layer_b = (w["wo_b"], w["lng"][1][0], w["lnb"][1][0], w["wup"][1], w["wdn"][1], w["lng"][1][1], w["lnb"][1][1])
    if prompt:
        kt_b, v_b = kv_bf16
        y = _attn_mlp_prompt(q3, kt_b, v_b.reshape(batch, seq, D_MODEL), x2.reshape(batch, seq, D_MODEL),
                             w["lam"], w["gn_b"], *layer_b)
        k_out = k_sh.reshape(batch, DIFF_HEADS, 2, DIFF_HD, seq).transpose(0, 4, 1, 2, 3)
    else:
        past = cache_k.shape[1]
        cache_kt = cache_k.transpose(0, 2, 3, 4, 1).reshape(batch, D_MODEL, past)
        o = _diff_attn_sample(q3, k_sh.reshape(batch, seq, D_MODEL), v3, cache_kt, cache_v, w["lam"], w["gn_b"])
        k_out = k_sh.reshape(batch, seq, DIFF_HEADS, 2, DIFF_HD)
        y = _out_mlp(o.reshape(n, D_MODEL), x2, *layer_b)
    return (y.reshape(batch, seq, D_MODEL), state[None], k_out, v3.reshape(batch, seq, DIFF_HEADS, HEAD_W))


def kernel(x_prompt, x_sample, state_gla, cache_k, cache_v, w_in_a, w_gate_up_a, b_gate_a, g_norm_a, w_o_a, w_kv, w_q_b, lam_b, g_norm_b, w_o_b, w_up, w_down, ln_g, ln_b):
    w = _prepare_weights(w_in_a, w_gate_up_a, b_gate_a, g_norm_a, w_o_a, w_kv, w_q_b, lam_b, g_norm_b,
                         w_o_b, w_up, w_down, ln_g, ln_b)
    batch, seq, _ = x_prompt.shape
    past = cache_k.shape[1]
    zero_state = jnp.zeros((batch, GLA_HEADS, GLA_DKH, GLA_DVH), F32)
    sample_out = _trunk(x_sample, past + jnp.arange(x_sample.shape[1]), state_gla[0], cache_k, cache_v, w)
    x_prompt, (y_s, s_s, k_s, v_s) = lax.optimization_barrier((x_prompt, sample_out))
    y_p, s_p, k_p, v_p = _trunk(x_prompt, jnp.arange(seq), zero_state, None, None, w)
    return (y_p, y_s, s_p, k_p, v_p, s_s, k_s, v_s)
```

```python
import functools
import math

import jax
import jax.numpy as jnp
from jax import lax
from jax.experimental import pallas as pl
from jax.experimental.pallas import tpu as pltpu

F32 = jnp.float32
BF16 = jnp.bfloat16

D_MODEL = 1024
DEPTH = 2
CHUNK = 64
GLA_HEADS = 4
GLA_DK = D_MODEL // 2
GLA_DV = D_MODEL
GLA_DKH = GLA_DK // GLA_HEADS
GLA_DVH = GLA_DV // GLA_HEADS
GATE_RANK = 16
GATE_TAU = 16.0
DIFF_HEADS = 8
DIFF_HD = D_MODEL // (2 * DIFF_HEADS)
HEAD_W = 2 * DIFF_HD
ROPE_DIMS = DIFF_HD // 4
ROPE_HALF = ROPE_DIMS // 2
ROPE_THETA = 500000.0
D_FF = 4 * D_MODEL
DEEPNORM_ALPHA = (2 * DEPTH) ** 0.25
LN_EPS = 1e-5
RMS_EPS = 1e-5
LAMBDA_INIT = 0.8 - 0.6 * math.exp(-0.3 * 1)
LOG2E = math.log2(math.e)

LANES = 128
SUBLANES = 8
MXU_COLS = 256
VMEM_LIMIT = 56 * 1024 * 1024
TOKEN_TILE = 512
TOKEN_PARTS = 2
NORM_ROWS = 64
GLA_TILE = 512
GLA_PARTS = 2
GLA_SAMPLE_ROWS = 128
ATT_TILE = 256
ATT_TILES_IN_FLIGHT = 3
FF_CHUNK = 1024

assert ROPE_HALF == SUBLANES


def _dot(a, b):
    return jnp.dot(a, b, preferred_element_type=F32)


def _dot_nt(a, b):
    return lax.dot_general(a, b, (((1,), (1,)), ((), ())), preferred_element_type=F32)


def _layer_norm(z, g, b):
    mu = jnp.mean(z, axis=-1, keepdims=True)
    zc = z - mu
    var = jnp.mean(zc * zc, axis=-1, keepdims=True)
    return zc * lax.rsqrt(var + LN_EPS) * g + b


def _rms_scale(x):
    return lax.rsqrt(jnp.mean(x * x, axis=-1, keepdims=True) + RMS_EPS)


def _resident(shape):
    zeros = (0,) * len(shape)
    return pl.BlockSpec(shape, lambda *_: zeros, pipeline_mode=pl.Buffered(1))


def _interleave(*lists):
    keyed = [(j / len(units), i, j, unit)
             for i, units in enumerate(lists) for j, unit in enumerate(units)]
    return [unit for _, _, _, unit in sorted(keyed, key=lambda t: t[:3])]


def _params(semantics):
    return pltpu.CompilerParams(dimension_semantics=semantics, vmem_limit_bytes=VMEM_LIMIT)


def _gla_layer_kernel(x_ref, si_ref, wm_ref, wgl_ref, wg_ref, bg_ref, gn_ref, wo_ref, lng_ref, lnb_ref,
                      x1_ref, so_ref, qf_scr, kf_scr, k_scr, b_scr, v_scr, r_scr, o_scr, *, nb, tile, chunk):
    @pl.when(pl.program_id(1) == 0)
    def _():
        so_ref[...] = si_ref[...]

    ci = lax.broadcasted_iota(jnp.int32, (chunk, chunk), 0)
    cj = lax.broadcasted_iota(jnp.int32, (chunk, chunk), 1)
    causal = cj <= ci
    part = nb * tile // GLA_PARTS
    row_in_chunk = lax.broadcasted_iota(jnp.int32, (part, MXU_COLS), 0) & (chunk - 1)

    def projection_units(rs):
        st = {}

        def cast():
            st["xb"] = x_ref[rs, :].astype(BF16)

        def gate_down():
            st["gl"] = _dot(st["xb"], wgl_ref[...]).astype(BF16)

        def gate_up(c0):
            def run():
                logit = _dot(st["gl"], wg_ref[:, c0:c0 + MXU_COLS]) + bg_ref[:, c0:c0 + MXU_COLS]
                log_sig = jnp.minimum(logit, 0.0) - jnp.log(1.0 + jnp.exp(-jnp.abs(logit)))
                st.setdefault("b", []).append(log_sig * (1.0 / GATE_TAU))
            return run

        def proj(c0):
            def run():
                y = _dot(st["xb"], wm_ref[:, c0:c0 + MXU_COLS])
                if c0 < GLA_DK:
                    st.setdefault("q", []).append(y * (GLA_DKH ** -0.5))
                elif c0 < 2 * GLA_DK:
                    st.setdefault("k", []).append(y)
                elif c0 < 2 * GLA_DK + GLA_DV:
                    v_scr[rs, c0 - 2 * GLA_DK:c0 - 2 * GLA_DK + MXU_COLS] = y.astype(BF16)
                else:
                    c = c0 - 2 * GLA_DK - GLA_DV
                    r_scr[rs, c:c + MXU_COLS] = y
            return run

        n_main = 2 * GLA_DK + 2 * GLA_DV
        matmuls = [proj(c0) for c0 in range(0, n_main, MXU_COLS)]
        gates = [gate_up(c0) for c0 in range(0, GLA_DK, MXU_COLS)]
        return [cast, gate_down] + _interleave(matmuls, gates), st

    def decay_units(rs, st):
        def scan_step(g, s):
            def run():
                b = st["b"][g]
                st["b"][g] = b + jnp.where(row_in_chunk >= s, pltpu.roll(b, s, axis=0), 0.0)
            return run

        def fold(g):
            def run():
                cols = slice(g * MXU_COLS, (g + 1) * MXU_COLS)
                q, k, b = st["q"][g], st["k"][g], st["b"][g]
                qf_scr[rs, cols] = (q * jnp.exp(b)).astype(BF16)
                kf_scr[rs, cols] = (k * jnp.exp(-b)).astype(BF16)
                k_scr[rs, cols] = k
                b_scr[rs, cols] = b
            return run

        shifts = [1 << p for p in range(chunk.bit_length() - 1)]
        groups = range(GLA_DK // MXU_COLS)
        return [scan_step(g, s) for s in shifts for g in groups] + [fold(g) for g in groups]

    def recurrence_units(row0):
        seqs = range(row0 // tile, (row0 + part - 1) // tile + 1)
        items = [(bi, r0, h) for bi in seqs
                 for r0 in range(max(row0, bi * tile), min(row0 + part, (bi + 1) * tile), chunk)
                 for h in range(GLA_HEADS)]
        chunk_vals = {}
        live = {}

        def chunk_prelude(r0):
            rows = slice(r0, r0 + chunk)
            b_c = b_scr[rows, :]
            b_last = b_c[chunk - 1:chunk, :]
            k_tail = (k_scr[rows, :] * jnp.exp(b_last - b_c)).astype(BF16)
            decay_rows = jnp.broadcast_to(jnp.exp(b_last), (SUBLANES, GLA_DK))
            return rows, k_tail, decay_rows

        def stage_scores(n):
            _, r0, h = items[n]
            if h == 0:
                chunk_vals[r0] = chunk_prelude(r0)
            rows, k_tail, decay_rows = chunk_vals[r0]
            ks = slice(h * GLA_DKH, (h + 1) * GLA_DKH)
            vs = slice(h * GLA_DVH, (h + 1) * GLA_DVH)
            qh = qf_scr[rows, ks]
            vh = v_scr[rows, vs]
            live[n] = dict(
                rows=rows, vs=vs, qh=qh, vh=vh,
                scores=_dot_nt(qh, kf_scr[rows, ks]),
                update=lax.dot_general(k_tail[:, ks], vh, (((0,), (0,)), ((), ())),
                                       preferred_element_type=F32),
                decay=decay_rows[:, ks].T[:, 0:1])

        def stage_outputs(n):
            bi, _, h = items[n]
            it = live[n]
            it["state"] = so_ref[bi, h]
            att = jnp.where(causal, it["scores"], 0.0).astype(BF16)
            it["o"] = _dot(it["qh"], it["state"].astype(BF16)) + _dot(att, it["vh"])

        def stage_commit(n):
            bi, _, h = items[n]
            it = live.pop(n)
            o_scr[it["rows"], it["vs"]] = it["o"]
            so_ref[bi, h] = it["state"] * it["decay"] + it["update"]

        stages = (stage_scores, stage_outputs, stage_commit)
        return [functools.partial(stage, step - depth)
                for step in range(len(items) + len(stages) - 1)
                for depth, stage in enumerate(stages) if 0 <= step - depth < len(items)]

    def output_units(rs):
        st = {"parts": []}

        def norm_head(h):
            def run():
                oh = o_scr[rs, h * GLA_DVH:(h + 1) * GLA_DVH]
                st["parts"].append(oh * _rms_scale(oh) * gn_ref[...])
            return run

        def gate():
            r = r_scr[rs, :]
            gated = jnp.concatenate(st.pop("parts"), axis=-1) * (r * (1.0 / (1.0 + jnp.exp(-r))))
            st["gated"] = gated.astype(BF16)
            st["mix"] = []

        def project(c0):
            def run():
                st["mix"].append(_dot(st["gated"], wo_ref[:, c0:c0 + MXU_COLS]))
            return run

        def residual_norm():
            mix = jnp.concatenate(st.pop("mix"), axis=-1)
            x1_ref[rs, :] = _layer_norm(DEEPNORM_ALPHA * x_ref[rs, :] + mix, lng_ref[...], lnb_ref[...])

        return ([norm_head(h) for h in range(GLA_HEADS)] + [gate]
                + [project(c0) for c0 in range(0, D_MODEL, MXU_COLS)] + [residual_norm])

    states = {}

    def projection_phase(i):
        units, states[i] = projection_units(slice(i * part, (i + 1) * part))
        return units

    def decay_phase(i):
        return decay_units(slice(i * part, (i + 1) * part), states.pop(i))

    def recurrence_phase(i):
        return recurrence_units(i * part)

    def output_phase(i):
        return output_units(slice(i * part, (i + 1) * part))

    _run_pipeline(GLA_PARTS, (projection_phase, decay_phase, recurrence_phase, output_phase))


def _gla_layer(x2d, state_in, wm, wgl, wg, bg, gn, wo, lng, lnb, *, batch, seq, chunk):
    n = x2d.shape[0]
    tile = min(GLA_TILE, seq)
    nt = seq // tile
    nb = max(1, min(GLA_SAMPLE_ROWS // seq, batch)) if tile == seq else 1
    m = nb * tile
    row = pl.BlockSpec((m, D_MODEL), lambda g, t: (g * nt + t, 0))
    st = pl.BlockSpec((nb, GLA_HEADS, GLA_DKH, GLA_DVH), lambda g, t: (g, 0, 0, 0))
    return pl.pallas_call(
        functools.partial(_gla_layer_kernel, nb=nb, tile=tile, chunk=chunk),
        grid=(batch // nb, nt),
        in_specs=[row, st, _resident(wm.shape), _resident(wgl.shape), _resident(wg.shape), _resident(bg.shape),
                  _resident(gn.shape), _resident(wo.shape), _resident(lng.shape), _resident(lnb.shape)],
        out_specs=[row, st],
        out_shape=[jax.ShapeDtypeStruct((n, D_MODEL), F32),
                   jax.ShapeDtypeStruct((batch, GLA_HEADS, GLA_DKH, GLA_DVH), F32)],
        scratch_shapes=[pltpu.VMEM((m, GLA_DK), BF16), pltpu.VMEM((m, GLA_DK), BF16),
                        pltpu.VMEM((m, GLA_DK), F32), pltpu.VMEM((m, GLA_DK), F32),
                        pltpu.VMEM((m, GLA_DV), BF16), pltpu.VMEM((m, GLA_DV), F32),
                        pltpu.VMEM((m, GLA_DV), F32)],
        compiler_params=_params(("parallel", "arbitrary")),
        name="gla_layer",
    )(x2d, state_in, wm, wgl, wg, bg, gn, wo, lng, lnb)


def _mlp_units(st, wup_ref, wdn_ref):
    col_groups = range(FF_CHUNK // MXU_COLS)

    def up(c, g):
        def run():
            c0 = c * FF_CHUNK + g * MXU_COLS
            hid = jnp.maximum(_dot(st["xb"], wup_ref[:, c0:c0 + MXU_COLS]), 0.0)
            st.setdefault("hid", {})[g] = (hid * hid).astype(BF16)
        return run

    def down(c, j):
        def run():
            if j == 0:
                st["hid_chunk"] = jnp.concatenate([st["hid"].pop(g) for g in col_groups], axis=-1)
            p = _dot(st["hid_chunk"], wdn_ref[c * FF_CHUNK:(c + 1) * FF_CHUNK, j * MXU_COLS:(j + 1) * MXU_COLS])
            y = st.setdefault("y", {})
            y[j] = p if c == 0 else y[j] + p
        return run

    units = []
    for c in range(D_FF // FF_CHUNK):
        units += [up(c, g) for g in col_groups] + [down(c, j) for j in range(D_MODEL // MXU_COLS)]
    return units


def _norm_units(rs, src, sink):
    step = min(NORM_ROWS, rs.stop - rs.start)

    def unit(r0):
        def run():
            rows = slice(r0, r0 + step)
            sink(rows, src(rows))
        return run
    return [unit(r0) for r0 in range(rs.start, rs.stop, step)]


def _pipeline_units(parts, phases):
    units = []
    for s in range(parts + len(phases) - 1):
        live = [phase(s - p) for p, phase in enumerate(phases) if 0 <= s - p < parts]
        units += _interleave(*live)
    return units


def _run_pipeline(parts, phases):
    for unit in _pipeline_units(parts, phases):
        unit()


def _rope(x, cos, sin):
    lane = lax.broadcasted_iota(jnp.int32, (x.shape[0], LANES), 1) & (DIFF_HD - 1)
    first = lane < ROPE_HALF
    second = (lane >= ROPE_HALF) & (lane < ROPE_DIMS)
    parts = []
    for j in range(x.shape[1] // LANES):
        xj = x[:, j * LANES:(j + 1) * LANES]
        ahead = pltpu.roll(xj, LANES - ROPE_HALF, axis=1)
        behind = pltpu.roll(xj, ROPE_HALF, axis=1)
        parts.append(jnp.where(first, xj * cos - ahead * sin,
                               jnp.where(second, xj * cos + behind * sin, xj)))
    return jnp.concatenate(parts, axis=-1)


def _rope_feature_major(xt, cos, sin):
    pieces = []
    for m in range(xt.shape[0] // DIFF_HD):
        base = m * DIFF_HD
        a = xt[base:base + ROPE_HALF]
        b = xt[base + ROPE_HALF:base + ROPE_DIMS]
        pieces += [a * cos - b * sin, b * cos + a * sin, xt[base + ROPE_DIMS:base + DIFF_HD]]
    return jnp.concatenate(pieces, axis=0)


def _mlp_kv_kernel(x1_ref, wup_ref, wdn_ref, lng_ref, lnb_ref, wk_ref, wv_ref, wq_ref, cos_ref, sin_ref,
                   kcos_ref, ksin_ref, x2_ref, k_ref, v_ref, q_ref, *rest, feature_major_k, parts):
    if feature_major_k:
        kb_ref, vb_ref, x2b_scr = rest
    else:
        (x2b_scr,) = rest
    part = x1_ref.shape[0] // parts
    states = {}
    col_starts = range(0, D_MODEL, MXU_COLS)

    def mlp_phase(i):
        rs = slice(i * part, (i + 1) * part)
        st = states[i] = {}

        def start():
            st["xb"] = x1_ref[rs, :].astype(BF16)

        return [start] + _mlp_units(st, wup_ref, wdn_ref)

    def norm_phase(i):
        rs = slice(i * part, (i + 1) * part)
        st = states.pop(i)

        def gather():
            st["y"] = jnp.concatenate([st["y"][j] for j in range(len(col_starts))], axis=-1)

        def src(rows):
            local = slice(rows.start - rs.start, rows.stop - rs.start)
            return _layer_norm(DEEPNORM_ALPHA * x1_ref[rows, :] + st["y"][local], lng_ref[...], lnb_ref[...])

        def sink(rows, x2):
            x2_ref[rows, :] = x2
            x2b_scr[rows, :] = x2.astype(BF16)

        return [gather] + _norm_units(rs, src, sink)

    def proj_phase(i):
        rs = slice(i * part, (i + 1) * part)
        st = {}

        def start():
            st["xb"] = x2b_scr[rs, :]

        def proj_k(c0):
            def run():
                cs = slice(c0, c0 + MXU_COLS)
                if feature_major_k:
                    kt = _rope_feature_major(_dot_nt(wk_ref[cs, :], st["xb"]), kcos_ref[:, rs], ksin_ref[:, rs])
                    k_ref[0, cs, rs] = kt
                    kb_ref[0, cs, rs] = kt.astype(BF16)
                else:
                    k_ref[rs, cs] = _rope(_dot(st["xb"], wk_ref[:, cs]), kcos_ref[rs, :], ksin_ref[rs, :])
            return run

        def proj_v(c0):
            def run():
                v = _dot(st["xb"], wv_ref[:, c0:c0 + MXU_COLS])
                v_ref[rs, c0:c0 + MXU_COLS] = v
                if feature_major_k:
                    vb_ref[rs, c0:c0 + MXU_COLS] = v.astype(BF16)
            return run

        def proj_q(c0):
            def run():
                q = _rope(_dot(st["xb"], wq_ref[:, c0:c0 + MXU_COLS]), cos_ref[rs, :], sin_ref[rs, :])
                q_ref[rs, c0:c0 + MXU_COLS] = (q * (DIFF_HD ** -0.5 * LOG2E)).astype(BF16)
            return run

        return [start] + [f(c0) for c0 in col_starts for f in (proj_k, proj_v, proj_q)]

    _run_pipeline(parts, (mlp_phase, norm_phase, proj_phase))


def _mlp_kv(x1, wup, wdn, lng, lnb, wk, wv, wq, tables, *, batch, seq, feature_major_k):
    n = x1.shape[0]
    tm = min(TOKEN_TILE, n)
    cos, sin, cos_t, sin_t = tables
    if tm > seq:
        cos = jnp.tile(cos, (tm // seq, 1))
        sin = jnp.tile(sin, (tm // seq, 1))
    npos = max(seq // tm, 1)
    row = lambda w: pl.BlockSpec((tm, w), lambda i: (i, 0))
    pos = pl.BlockSpec((tm, LANES), lambda i: (i % npos, 0))
    if feature_major_k:
        kcos, ksin = cos_t, sin_t
        kpos = pl.BlockSpec((ROPE_HALF, tm), lambda i: (0, i % npos))
        k_spec = pl.BlockSpec((1, D_MODEL, tm), lambda i: (i // npos, 0, i % npos))
        k_shape = jax.ShapeDtypeStruct((batch, D_MODEL, seq), F32)
    else:
        kcos, ksin, kpos = cos, sin, pos
        k_spec = row(D_MODEL)
        k_shape = jax.ShapeDtypeStruct((n, D_MODEL), F32)
    out_specs = [row(D_MODEL), k_spec, row(D_MODEL), row(D_MODEL)]
    out_shape = [jax.ShapeDtypeStruct((n, D_MODEL), F32), k_shape,
                 jax.ShapeDtypeStruct((n, D_MODEL), F32), jax.ShapeDtypeStruct((n, D_MODEL), BF16)]
    if feature_major_k:
        out_specs += [k_spec, row(D_MODEL)]
        out_shape += [jax.ShapeDtypeStruct(k_shape.shape, BF16), jax.ShapeDtypeStruct((n, D_MODEL), BF16)]
    return pl.pallas_call(
        functools.partial(_mlp_kv_kernel, feature_major_k=feature_major_k, parts=TOKEN_PARTS),
        grid=(n // tm,),
        in_specs=[row(D_MODEL), _resident(wup.shape), _resident(wdn.shape), _resident(lng.shape),
                  _resident(lnb.shape), _resident(wk.shape), _resident(wv.shape), _resident(wq.shape),
                  pos, pos, kpos, kpos],
        out_specs=out_specs,
        out_shape=out_shape,
        scratch_shapes=[pltpu.VMEM((tm, D_MODEL), BF16)],
        compiler_params=_params(("parallel",)),
        name="mlp_kv",
    )(x1, wup, wdn, lng, lnb, wk, wv, wq, cos, sin, kcos, ksin)


def _out_mlp_kernel(o_ref, x2_ref, wo_ref, lng1_ref, lnb1_ref, wup_ref, wdn_ref, lng2_ref, lnb2_ref, y_ref,
                    x3_scr, x3b_scr, *, parts):
    part = o_ref.shape[0] // parts
    states = {}
    col_starts = range(0, D_MODEL, MXU_COLS)

    def mix_phase(i):
        rs = slice(i * part, (i + 1) * part)
        st = {"mix": []}

        def project(c0):
            def run():
                st["mix"].append(_dot(o_ref[rs, :], wo_ref[:, c0:c0 + MXU_COLS]))
            return run

        def gather():
            st["mix"] = jnp.concatenate(st["mix"], axis=-1)

        def src(rows):
            local = slice(rows.start - rs.start, rows.stop - rs.start)
            return _layer_norm(DEEPNORM_ALPHA * x2_ref[rows, :] + st["mix"][local], lng1_ref[...], lnb1_ref[...])

        def sink(rows, x3):
            x3_scr[rows, :] = x3
            x3b_scr[rows, :] = x3.astype(BF16)

        return [project(c0) for c0 in col_starts] + [gather] + _norm_units(rs, src, sink)

    def mlp_phase(i):
        rs = slice(i * part, (i + 1) * part)
        st = states[i] = {}

        def start():
            st["xb"] = x3b_scr[rs, :]

        return [start] + _mlp_units(st, wup_ref, wdn_ref)

    def final_phase(i):
        rs = slice(i * part, (i + 1) * part)
        st = states.pop(i)

        def gather():
            st["y"] = jnp.concatenate([st["y"][j] for j in range(len(col_starts))], axis=-1)

        def src(rows):
            local = slice(rows.start - rs.start, rows.stop - rs.start)
            return _layer_norm(DEEPNORM_ALPHA * x3_scr[rows, :] + st["y"][local], lng2_ref[...], lnb2_ref[...])

        def sink(rows, y):
            y_ref[rows, :] = y

        return [gather] + _norm_units(rs, src, sink)

    _run_pipeline(parts, (mix_phase, mlp_phase, final_phase))


def _out_mlp(o, x2, wo, lng1, lnb1, wup, wdn, lng2, lnb2):
    n = x2.shape[0]
    tm = min(TOKEN_TILE, n)
    row = lambda w: pl.BlockSpec((tm, w), lambda i: (i, 0))
    return pl.pallas_call(
        functools.partial(_out_mlp_kernel, parts=TOKEN_PARTS),
        grid=(n // tm,),
        in_specs=[row(D_MODEL), row(D_MODEL), _resident(wo.shape), _resident(lng1.shape),
                  _resident(lnb1.shape), _resident(wup.shape), _resident(wdn.shape),
                  _resident(lng2.shape), _resident(lnb2.shape)],
        out_specs=row(D_MODEL),
        out_shape=jax.ShapeDtypeStruct((n, D_MODEL), F32),
        scratch_shapes=[pltpu.VMEM((tm, D_MODEL), F32), pltpu.VMEM((tm, D_MODEL), BF16)],
        compiler_params=_params(("parallel",)),
        name="out_mlp",
    )(o, x2, wo, lng1, lnb1, wup, wdn, lng2, lnb2)


def _lambda_full(lam):
    t1 = jnp.sum(lam[0:1, :] * lam[1:2, :], axis=-1, keepdims=True)
    t2 = jnp.sum(lam[2:3, :] * lam[3:4, :], axis=-1, keepdims=True)
    return jnp.exp(t1) - jnp.exp(t2) + LAMBDA_INIT


def _stack_maps(q):
    qf = q.astype(F32)
    lane = lax.broadcasted_iota(jnp.int32, qf.shape, 1)
    return jnp.concatenate([jnp.where(lane < DIFF_HD, qf, 0.0), jnp.where(lane >= DIFF_HD, qf, 0.0)],
                           axis=0).astype(BF16)


def _head_norm(acc, gn):
    return (acc * _rms_scale(acc) * gn * (1.0 - LAMBDA_INIT)).astype(BF16)


def _lane_fold(x, op):
    out = x[:, 0:LANES]
    for j in range(1, x.shape[1] // LANES):
        out = op(out, x[:, j * LANES:(j + 1) * LANES])
    return out


def _attention_units(q_ref, kt_ref, v_ref, s_scr, lam, gn, store, *, tq, order):
    chunk_shift = CHUNK.bit_length() - 1
    rc = (lax.broadcasted_iota(jnp.int32, (2 * tq, tq), 0) & (tq - 1)) >> chunk_shift
    cc = lax.broadcasted_iota(jnp.int32, (2 * tq, tq), 1) >> chunk_shift
    visible = cc <= rc

    states = {}
    slots = s_scr.shape[0]

    def score_phase(pos):
        i = order[pos]
        st = states[pos] = {}
        slot = pos % slots

        def start():
            st["qq"] = _stack_maps(q_ref[0, i * tq:(i + 1) * tq, :])

        def block(j):
            def run():
                s = _dot(st["qq"], kt_ref[0, :, j * tq:(j + 1) * tq])
                if j == i:
                    s = jnp.where(visible, s, -jnp.inf)
                s_scr[slot, j] = s
                f = _lane_fold(s, jnp.maximum)
                st["m"] = f if j == 0 else jnp.maximum(st["m"], f)
            return run

        def finish():
            st["m"] = jnp.max(st["m"], axis=-1, keepdims=True)

        return [start] + [block(j) for j in range(i + 1)] + [finish]

    def exp_phase(pos):
        i = order[pos]
        st = states[pos]
        slot = pos % slots

        def exp_block(j):
            def run():
                e = jnp.exp2(s_scr[slot, j] - st["m"])
                s_scr[slot, j] = e
                f = _lane_fold(e, jnp.add)
                st["l"] = f if j == 0 else st["l"] + f
            return run

        def weights():
            l = jnp.sum(st["l"], axis=-1, keepdims=True)
            st["inv_l0"] = 1.0 / l[0:tq]
            st["rho"] = lam * l[0:tq] / l[tq:]

        return [exp_block(j) for j in range(i + 1)] + [weights]

    def value_phase(pos):
        i = order[pos]
        st = states.pop(pos)
        slot = pos % slots

        def pv_block(j):
            def run():
                e = s_scr[slot, j]
                a = (e[0:tq] - e[tq:] * st["rho"]).astype(BF16)
                p = _dot(a, v_ref[0, j * tq:(j + 1) * tq, :])
                st["acc"] = p if j == 0 else st["acc"] + p
            return run

        def finish():
            store(i, _head_norm(st["acc"] * st["inv_l0"], gn))

        return [pv_block(j) for j in range(i + 1)] + [finish]

    return _pipeline_units(len(order), (score_phase, exp_phase, value_phase))


def _attn_mlp_kernel(lam_ref, gn_ref, q_ref, kt_ref, v_ref, x2_ref, wo_ref, lng1_ref, lnb1_ref, wup_ref, wdn_ref,
                     lng2_ref, lnb2_ref, y_ref, s_scr, o_scr, x3_scr, x3b_scr, *, tq, nq, batch):
    b = pl.program_id(0)
    h = pl.program_id(1)
    cur = b % 2
    part = x2_ref.shape[1]

    def attention(order):
        def store(i, o):
            o_scr[cur, h, i * tq:(i + 1) * tq, :] = o

        return _attention_units(q_ref, kt_ref, v_ref, s_scr, _lambda_full(lam_ref[...]), gn_ref[...], store,
                                tq=tq, order=order)

    growing = list(range(nq))
    shrinking = growing[::-1]

    def channel_mixer():
        r0 = pl.multiple_of(h * part, part)
        st = {"mix": []}
        col_starts = range(0, D_MODEL, MXU_COLS)

        def gather_heads():
            st["o"] = jnp.concatenate([o_scr[1 - cur, g, pl.ds(r0, part), :] for g in range(DIFF_HEADS)], axis=-1)

        def project(c0):
            def run():
                st["mix"].append(_dot(st["o"], wo_ref[:, c0:c0 + MXU_COLS]))
            return run

        def gather_mix():
            st["mix"] = jnp.concatenate(st["mix"], axis=-1)

        def norm1_src(rows):
            return _layer_norm(DEEPNORM_ALPHA * x2_ref[0, rows, :] + st["mix"][rows], lng1_ref[...], lnb1_ref[...])

        def norm1_sink(rows, x3):
            x3_scr[rows, :] = x3
            x3b_scr[rows, :] = x3.astype(BF16)

        def load():
            st["xb"] = x3b_scr[...]

        def gather_y():
            st["y"] = jnp.concatenate([st["y"][j] for j in range(len(col_starts))], axis=-1)

        def norm2_src(rows):
            return _layer_norm(DEEPNORM_ALPHA * x3_scr[rows, :] + st["y"][rows], lng2_ref[...], lnb2_ref[...])

        def norm2_sink(rows, y):
            y_ref[0, rows, :] = y

        whole = slice(0, part)
        return ([gather_heads] + [project(c0) for c0 in col_starts] + [gather_mix]
                + _norm_units(whole, norm1_src, norm1_sink) + [load] + _mlp_units(st, wup_ref, wdn_ref)
                + [gather_y] + _norm_units(whole, norm2_src, norm2_sink))

    @pl.when(b == 0)
    def _():
        for unit in attention(growing):
            unit()

    @pl.when((b > 0) & (b < batch))
    def _():
        for unit in _interleave(attention(shrinking), channel_mixer()):
            unit()

    @pl.when(b == batch)
    def _():
        for unit in channel_mixer():
            unit()


def _attn_mlp_prompt(q, kt, v, x2, lam, gn, wo, lng1, lnb1, wup, wdn, lng2, lnb2):
    batch, seq, _ = q.shape
    tq = min(ATT_TILE, seq)
    nq = seq // tq
    part = seq // DIFF_HEADS
    last = batch - 1
    rows = pl.BlockSpec((1, seq, HEAD_W), lambda b, h: (jnp.minimum(b, last), 0, h))
    cols = pl.BlockSpec((1, HEAD_W, seq), lambda b, h: (jnp.minimum(b, last), h, 0))
    lagged = pl.BlockSpec((1, part, D_MODEL),
                          lambda b, h: (jnp.maximum(b - 1, 0), jnp.where(b == 0, 0, h), 0))
    weights = [wo, lng1, lnb1, wup, wdn, lng2, lnb2]
    return pl.pallas_call(
        functools.partial(_attn_mlp_kernel, tq=tq, nq=nq, batch=batch),
        grid=(batch + 1, DIFF_HEADS),
        in_specs=[_resident(lam.shape), _resident(gn.shape), rows, cols, rows, lagged]
                 + [_resident(w.shape) for w in weights],
        out_specs=lagged,
        out_shape=jax.ShapeDtypeStruct((batch, seq, D_MODEL), F32),
        scratch_shapes=[pltpu.VMEM((ATT_TILES_IN_FLIGHT, nq, 2 * tq, tq), F32),
                        pltpu.VMEM((2, DIFF_HEADS, seq, HEAD_W), BF16),
                        pltpu.VMEM((part, D_MODEL), F32), pltpu.VMEM((part, D_MODEL), BF16)],
        compiler_params=_params(("arbitrary", "arbitrary")),
        name="attn_mlp",
    )(lam, gn, q, kt, v, x2, *weights)


def _attn_sample_kernel(lam_ref, gn_ref, q_ref, ckt_ref, cv_ref, kn_ref, vn_ref, o_ref):
    lam = _lambda_full(lam_ref[...])
    gn = gn_ref[...]
    t = q_ref.shape[1]
    past = ckt_ref.shape[2]
    live = {}

    def stage_scores(h):
        hs = slice(h * HEAD_W, (h + 1) * HEAD_W)
        qq = _stack_maps(q_ref[0, :, hs])
        live[h] = dict(hs=hs, sp=_dot(qq, ckt_ref[0, hs, :].astype(BF16)),
                       sn=_dot_nt(qq, kn_ref[0, :, hs].astype(BF16)))

    def stage_softmax(h):
        it = live[h]
        sp, sn = it.pop("sp"), it.pop("sn")
        m = jnp.maximum(jnp.max(sp, axis=-1, keepdims=True), jnp.max(sn, axis=-1, keepdims=True))
        ep = jnp.exp2(sp - m)
        en = jnp.exp2(sn - m)
        l = jnp.sum(ep, axis=-1, keepdims=True) + jnp.sum(en, axis=-1, keepdims=True)
        rho = lam * l[0:t] / l[t:]
        it["ap"] = (ep[0:t] - ep[t:] * rho).astype(BF16)
        it["an"] = (en[0:t] - en[t:] * rho).astype(BF16)
        it["inv_l0"] = 1.0 / l[0:t]

    def stage_values(h):
        it = live.pop(h)
        cv = cv_ref[0, pl.ds(h, past, stride=DIFF_HEADS), :]
        acc = _dot(it["ap"], cv.astype(BF16)) + _dot(it["an"], vn_ref[0, :, it["hs"]].astype(BF16))
        o_ref[0, :, it["hs"]] = _head_norm(acc * it["inv_l0"], gn)

    stages = (stage_scores, stage_softmax, stage_values)
    for step in range(DIFF_HEADS + len(stages) - 1):
        for depth, stage in enumerate(stages):
            if 0 <= step - depth < DIFF_HEADS:
                stage(step - depth)


def _diff_attn_sample(q, k_new, v_new, cache_kt, cache_v, lam, gn):
    batch, t, _ = q.shape
    past = cache_v.shape[1]
    new = pl.BlockSpec((1, t, D_MODEL), lambda b: (b, 0, 0))
    return pl.pallas_call(
        _attn_sample_kernel,
        grid=(batch,),
        in_specs=[_resident(lam.shape), _resident(gn.shape), new,
                  pl.BlockSpec((1, D_MODEL, past), lambda b: (b, 0, 0)),
                  pl.BlockSpec((1, past * DIFF_HEADS, HEAD_W), lambda b: (b, 0, 0)), new, new],
        out_specs=new,
        out_shape=jax.ShapeDtypeStruct((batch, t, D_MODEL), BF16),
        compiler_params=_params(("parallel",)),
        name="diff_attn_sample",
    )(lam, gn, q, cache_kt, cache_v.reshape(batch, past * DIFF_HEADS, HEAD_W), k_new, v_new)


def _rope_tables(pos):
    inv = 1.0 / (ROPE_THETA ** (jnp.arange(0, ROPE_DIMS, 2, dtype=F32) / ROPE_DIMS))
    ang = pos.astype(F32)[:, None] * inv[None, :]
    cos, sin = jnp.cos(ang), jnp.sin(ang)
    idx = jnp.arange(LANES) % ROPE_HALF
    return cos[:, idx], sin[:, idx], cos.T, sin.T


def _prepare_weights(w_in_a, w_gate_up_a, b_gate_a, g_norm_a, w_o_a, w_kv, w_q_b, lam_b, g_norm_b, w_o_b,
                     w_up, w_down, ln_g, ln_b):
    n_main = 2 * GLA_DK + 2 * GLA_DV
    w_in = w_in_a[0]
    row = lambda a: a.reshape(1, -1)
    wk = w_kv[:, :D_MODEL].astype(BF16)
    return dict(
        wm=w_in[:, :n_main].astype(BF16),
        wgl=jnp.pad(w_in[:, n_main:], ((0, 0), (0, LANES - GATE_RANK))).astype(BF16),
        wg=jnp.pad(w_gate_up_a[0], ((0, LANES - GATE_RANK), (0, 0))).astype(BF16),
        bg=row(b_gate_a[0]),
        gn_a=row(g_norm_a[0]),
        wo_a=w_o_a[0].astype(BF16),
        wk=wk,
        wk_t=wk.T,
        wv=w_kv[:, D_MODEL:].astype(BF16),
        wq=w_q_b[0].astype(BF16),
        lam=lam_b[0],
        gn_b=row(g_norm_b[0]),
        wo_b=w_o_b[0].astype(BF16),
        wup=[w_up[i].astype(BF16) for i in range(DEPTH)],
        wdn=[w_down[i].astype(BF16) for i in range(DEPTH)],
        lng=[[row(ln_g[i, j]) for j in range(2)] for i in range(DEPTH)],
        lnb=[[row(ln_b[i, j]) for j in range(2)] for i in range(DEPTH)],
    )


def _trunk(x, pos, state_in, cache_k, cache_v, w):
    batch, seq, _ = x.shape
    n = batch * seq
    prompt = cache_k is None
    x2d = x.reshape(n, D_MODEL)
    x1, state = _gla_layer(x2d, state_in, w["wm"], w["wgl"], w["wg"], w["bg"], w["gn_a"], w["wo_a"],
                           w["lng"][0][0], w["lnb"][0][0], batch=batch, seq=seq, chunk=min(CHUNK, seq))
    x2, k_sh, v_sh, q_b, *kv_bf16 = _mlp_kv(x1, w["wup"][0], w["wdn"][0], w["lng"][0][1], w["lnb"][0][1],
                                            w["wk_t"] if prompt else w["wk"], w["wv"], w["wq"],
                                            _rope_tables(pos), batch=batch, seq=seq, feature_major_k=prompt)
    v3 = v_sh.reshape(batch, seq, D_MODEL)
    q3 = q_b.reshape(batch, seq, D_MODEL)
    layer_b = (w["wo_b"], w["lng"][1][0], w["lnb"][1][0], w["wup"][1], w["wdn"][1], w["lng"][1][1], w["lnb"][1][1])
    if prompt:
        kt_b, v_b = kv_bf16
        y = _attn_mlp_prompt(q3, kt_b, v_b.reshape(batch, seq, D_MODEL), x2.reshape(batch, seq, D_MODEL),
                             w["lam"], w["gn_b"], *layer_b)
        k_out = k_sh.reshape(batch, DIFF_HEADS, 2, DIFF_HD, seq).transpose(0, 4, 1, 2, 3)
    else:
        past = cache_k.shape[1]
        cache_kt = cache_k.transpose(0, 2, 3, 4, 1).reshape(batch, D_MODEL, past)
        o = _diff_attn_sample(q3, k_sh.reshape(batch, seq, D_MODEL), v3, cache_kt, cache_v, w["lam"], w["gn_b"])
        k_out = k_sh.reshape(batch, seq, DIFF_HEADS, 2, DIFF_HD)
        y = _out_mlp(o.reshape(n, D_MODEL), x2, *layer_b)
    return (y.reshape(batch, seq, D_MODEL), state[None], k_out, v3.reshape(batch, seq, DIFF_HEADS, HEAD_W))


def kernel(x_prompt, x_sample, state_gla, cache_k, cache_v, w_in_a, w_gate_up_a, b_gate_a, g_norm_a, w_o_a, w_kv, w_q_b, lam_b, g_norm_b, w_o_b, w_up, w_down, ln_g, ln_b):
    w = _prepare_weights(w_in_a, w_gate_up_a, b_gate_a, g_norm_a, w_o_a, w_kv, w_q_b, lam_b, g_norm_b,
                         w_o_b, w_up, w_down, ln_g, ln_b)
    batch, seq, _ = x_prompt.shape
    past = cache_k.shape[1]
    zero_state = jnp.zeros((batch, GLA_HEADS, GLA_DKH, GLA_DVH), F32)
    sample_out = _trunk(x_sample, past + jnp.arange(x_sample.shape[1]), state_gla[0], cache_k, cache_v, w)
    x_prompt, (y_s, s_s, k_s, v_s) = lax.optimization_barrier((x_prompt, sample_out))
    y_p, s_p, k_p, v_p = _trunk(x_prompt, jnp.arange(seq), zero_state, None, None, w)
    return (y_p, y_s, s_p, k_p, v_p, s_s, k_s, v_s)
```

```python
import functools
import math

import jax
import jax.numpy as jnp
from jax import lax
from jax.experimental import pallas as pl
from jax.experimental.pallas import tpu as pltpu

F32 = jnp.float32
BF16 = jnp.bfloat16

D_MODEL = 1024
DEPTH = 2
CHUNK = 64
GLA_HEADS = 4
GLA_DK = D_MODEL // 2
GLA_DV = D_MODEL
GLA_DKH = GLA_DK // GLA_HEADS
GLA_DVH = GLA_DV // GLA_HEADS
GATE_RANK = 16
GATE_TAU = 16.0
DIFF_HEADS = 8
DIFF_HD = D_MODEL // (2 * DIFF_HEADS)
HEAD_W = 2 * DIFF_HD
ROPE_DIMS = DIFF_HD // 4
ROPE_HALF = ROPE_DIMS // 2
ROPE_THETA = 500000.0
D_FF = 4 * D_MODEL
DEEPNORM_ALPHA = (2 * DEPTH) ** 0.25
LN_EPS = 1e-5
RMS_EPS = 1e-5
LAMBDA_INIT = 0.8 - 0.6 * math.exp(-0.3 * 1)
LOG2E = math.log2(math.e)

LANES = 128
SUBLANES = 8
MXU_COLS = 256
VMEM_LIMIT = 56 * 1024 * 1024
TOKEN_TILE = 512
TOKEN_PARTS = 2
NORM_ROWS = 64
GLA_TILE = 1024
GLA_PARTS = 4
GLA_SAMPLE_ROWS = 128
ATT_TILE = 256
ATT_TILES_IN_FLIGHT = 3
FF_CHUNK = 1024

assert ROPE_HALF == SUBLANES


def _dot(a, b):
    return jnp.dot(a, b, preferred_element_type=F32)


def _dot_nt(a, b):
    return lax.dot_general(a, b, (((1,), (1,)), ((), ())), preferred_element_type=F32)


def _layer_norm(z, g, b):
    mu = jnp.mean(z, axis=-1, keepdims=True)
    zc = z - mu
    var = jnp.mean(zc * zc, axis=-1, keepdims=True)
    return zc * lax.rsqrt(var + LN_EPS) * g + b


def _rms_scale(x):
    return lax.rsqrt(jnp.mean(x * x, axis=-1, keepdims=True) + RMS_EPS)


def _resident(shape):
    zeros = (0,) * len(shape)
    return pl.BlockSpec(shape, lambda *_: zeros, pipeline_mode=pl.Buffered(1))


def _interleave(*lists):
    keyed = [(j / len(units), i, j, unit)
             for i, units in enumerate(lists) for j, unit in enumerate(units)]
    return [unit for _, _, _, unit in sorted(keyed, key=lambda t: t[:3])]


def _params(semantics):
    return pltpu.CompilerParams(dimension_semantics=semantics, vmem_limit_bytes=VMEM_LIMIT)


def _gla_layer_kernel(x_ref, si_ref, wm_ref, wgl_ref, wg_ref, bg_ref, gn_ref, wo_ref, lng_ref, lnb_ref,
                      x1_ref, so_ref, qf_scr, kf_scr, k_scr, b_scr, v_scr, r_scr, o_scr, *, nb, tile, chunk,
                      parts):
    @pl.when(pl.program_id(1) == 0)
    def _():
        so_ref[...] = si_ref[...]

    ci = lax.broadcasted_iota(jnp.int32, (chunk, chunk), 0)
    cj = lax.broadcasted_iota(jnp.int32, (chunk, chunk), 1)
    causal = cj <= ci
    part = nb * tile // parts
    row_in_chunk = lax.broadcasted_iota(jnp.int32, (part, MXU_COLS), 0) & (chunk - 1)

    def projection_units(rs):
        st = {}

        def cast():
            st["xb"] = x_ref[rs, :].astype(BF16)

        def gate_down():
            st["gl"] = _dot(st["xb"], wgl_ref[...]).astype(BF16)

        def gate_up(c0):
            def run():
                logit = _dot(st["gl"], wg_ref[:, c0:c0 + MXU_COLS]) + bg_ref[:, c0:c0 + MXU_COLS]
                log_sig = jnp.minimum(logit, 0.0) - jnp.log(1.0 + jnp.exp(-jnp.abs(logit)))
                st.setdefault("b", []).append(log_sig * (1.0 / GATE_TAU))
            return run

        def proj(c0):
            def run():
                y = _dot(st["xb"], wm_ref[:, c0:c0 + MXU_COLS])
                if c0 < GLA_DK:
                    st.setdefault("q", []).append(y * (GLA_DKH ** -0.5))
                elif c0 < 2 * GLA_DK:
                    st.setdefault("k", []).append(y)
                elif c0 < 2 * GLA_DK + GLA_DV:
                    v_scr[rs, c0 - 2 * GLA_DK:c0 - 2 * GLA_DK + MXU_COLS] = y.astype(BF16)
                else:
                    c = c0 - 2 * GLA_DK - GLA_DV
                    r_scr[rs, c:c + MXU_COLS] = y
            return run

        n_main = 2 * GLA_DK + 2 * GLA_DV
        matmuls = [proj(c0) for c0 in range(0, n_main, MXU_COLS)]
        gates = [gate_up(c0) for c0 in range(0, GLA_DK, MXU_COLS)]
        return [cast, gate_down] + _interleave(matmuls, gates), st

    def decay_units(rs, st):
        def scan_step(g, s):
            def run():
                b = st["b"][g]
                st["b"][g] = b + jnp.where(row_in_chunk >= s, pltpu.roll(b, s, axis=0), 0.0)
            return run

        def fold(g):
            def run():
                cols = slice(g * MXU_COLS, (g + 1) * MXU_COLS)
                q, k, b = st["q"][g], st["k"][g], st["b"][g]
                qf_scr[rs, cols] = (q * jnp.exp(b)).astype(BF16)
                kf_scr[rs, cols] = (k * jnp.exp(-b)).astype(BF16)
                k_scr[rs, cols] = k
                b_scr[rs, cols] = b
            return run

        shifts = [1 << p for p in range(chunk.bit_length() - 1)]
        groups = range(GLA_DK // MXU_COLS)
        return [scan_step(g, s) for s in shifts for g in groups] + [fold(g) for g in groups]

    def recurrence_units(row0):
        seqs = range(row0 // tile, (row0 + part - 1) // tile + 1)
        items = [(bi, r0, h) for bi in seqs
                 for r0 in range(max(row0, bi * tile), min(row0 + part, (bi + 1) * tile), chunk)
                 for h in range(GLA_HEADS)]
        chunk_vals = {}
        live = {}

        def chunk_prelude(r0):
            rows = slice(r0, r0 + chunk)
            b_c = b_scr[rows, :]
            b_last = b_c[chunk - 1:chunk, :]
            k_tail = (k_scr[rows, :] * jnp.exp(b_last - b_c)).astype(BF16)
            decay_rows = jnp.broadcast_to(jnp.exp(b_last), (SUBLANES, GLA_DK))
            return rows, k_tail, decay_rows

        def stage_scores(n):
            _, r0, h = items[n]
            if h == 0:
                chunk_vals[r0] = chunk_prelude(r0)
            rows, k_tail, decay_rows = chunk_vals[r0]
            ks = slice(h * GLA_DKH, (h + 1) * GLA_DKH)
            vs = slice(h * GLA_DVH, (h + 1) * GLA_DVH)
            qh = qf_scr[rows, ks]
            vh = v_scr[rows, vs]
            live[n] = dict(
                rows=rows, vs=vs, qh=qh, vh=vh,
                scores=_dot_nt(qh, kf_scr[rows, ks]),
                update=lax.dot_general(k_tail[:, ks], vh, (((0,), (0,)), ((), ())),
                                       preferred_element_type=F32),
                decay=decay_rows[:, ks].T[:, 0:1])

        def stage_outputs(n):
            bi, _, h = items[n]
            it = live[n]
            it["state"] = so_ref[bi, h]
            att = jnp.where(causal, it["scores"], 0.0).astype(BF16)
            it["o"] = _dot(it["qh"], it["state"].astype(BF16)) + _dot(att, it["vh"])

        def stage_commit(n):
            bi, _, h = items[n]
            it = live.pop(n)
            o_scr[it["rows"], it["vs"]] = it["o"]
            so_ref[bi, h] = it["state"] * it["decay"] + it["update"]

        stages = (stage_scores, stage_outputs, stage_commit)
        return [functools.partial(stage, step - depth)
                for step in range(len(items) + len(stages) - 1)
                for depth, stage in enumerate(stages) if 0 <= step - depth < len(items)]

    def output_units(rs):
        st = {"parts": []}

        def norm_head(h):
            def run():
                oh = o_scr[rs, h * GLA_DVH:(h + 1) * GLA_DVH]
                st["parts"].append(oh * _rms_scale(oh) * gn_ref[...])
            return run

        def gate():
            r = r_scr[rs, :]
            gated = jnp.concatenate(st.pop("parts"), axis=-1) * (r * (1.0 / (1.0 + jnp.exp(-r))))
            st["gated"] = gated.astype(BF16)
            st["mix"] = []

        def project(c0):
            def run():
                st["mix"].append(_dot(st["gated"], wo_ref[:, c0:c0 + MXU_COLS]))
            return run

        def residual_norm():
            mix = jnp.concatenate(st.pop("mix"), axis=-1)
            x1_ref[rs, :] = _layer_norm(DEEPNORM_ALPHA * x_ref[rs, :] + mix, lng_ref[...], lnb_ref[...])

        return ([norm_head(h) for h in range(GLA_HEADS)] + [gate]
                + [project(c0) for c0 in range(0, D_MODEL, MXU_COLS)] + [residual_norm])

    states = {}

    def projection_phase(i):
        units, states[i] = projection_units(slice(i * part, (i + 1) * part))
        return units

    def decay_phase(i):
        return decay_units(slice(i * part, (i + 1) * part), states.pop(i))

    def recurrence_phase(i):
        return recurrence_units(i * part)

    def output_phase(i):
        return output_units(slice(i * part, (i + 1) * part))

    _run_pipeline(parts, (projection_phase, decay_phase, recurrence_phase, output_phase))


def _gla_layer(x2d, state_in, wm, wgl, wg, bg, gn, wo, lng, lnb, *, batch, seq, chunk):
    n = x2d.shape[0]
    tile = min(GLA_TILE, seq)
    nt = seq // tile
    nb = max(1, min(GLA_SAMPLE_ROWS // seq, batch)) if tile == seq else 1
    m = nb * tile
    parts = max(p for p in (1, 2, 4, 8) if p <= GLA_PARTS and m % (p * chunk) == 0)
    row = pl.BlockSpec((m, D_MODEL), lambda g, t: (g * nt + t, 0))
    st = pl.BlockSpec((nb, GLA_HEADS, GLA_DKH, GLA_DVH), lambda g, t: (g, 0, 0, 0))
    return pl.pallas_call(
        functools.partial(_gla_layer_kernel, nb=nb, tile=tile, chunk=chunk, parts=parts),
        grid=(batch // nb, nt),
        in_specs=[row, st, _resident(wm.shape), _resident(wgl.shape), _resident(wg.shape), _resident(bg.shape),
                  _resident(gn.shape), _resident(wo.shape), _resident(lng.shape), _resident(lnb.shape)],
        out_specs=[row, st],
        out_shape=[jax.ShapeDtypeStruct((n, D_MODEL), F32),
                   jax.ShapeDtypeStruct((batch, GLA_HEADS, GLA_DKH, GLA_DVH), F32)],
        scratch_shapes=[pltpu.VMEM((m, GLA_DK), BF16), pltpu.VMEM((m, GLA_DK), BF16),
                        pltpu.VMEM((m, GLA_DK), F32), pltpu.VMEM((m, GLA_DK), F32),
                        pltpu.VMEM((m, GLA_DV), BF16), pltpu.VMEM((m, GLA_DV), F32),
                        pltpu.VMEM((m, GLA_DV), F32)],
        compiler_params=_params(("parallel", "arbitrary")),
        name="gla_layer",
    )(x2d, state_in, wm, wgl, wg, bg, gn, wo, lng, lnb)


def _mlp_units(st, wup_ref, wdn_ref):
    col_groups = range(FF_CHUNK // MXU_COLS)

    def up(c, g):
        def run():
            c0 = c * FF_CHUNK + g * MXU_COLS
            hid = jnp.maximum(_dot(st["xb"], wup_ref[:, c0:c0 + MXU_COLS]), 0.0)
            st.setdefault("hid", {})[g] = (hid * hid).astype(BF16)
        return run

    def down(c, j):
        def run():
            if j == 0:
                st["hid_chunk"] = jnp.concatenate([st["hid"].pop(g) for g in col_groups], axis=-1)
            p = _dot(st["hid_chunk"], wdn_ref[c * FF_CHUNK:(c + 1) * FF_CHUNK, j * MXU_COLS:(j + 1) * MXU_COLS])
            y = st.setdefault("y", {})
            y[j] = p if c == 0 else y[j] + p
        return run

    units = []
    for c in range(D_FF // FF_CHUNK):
        units += [up(c, g) for g in col_groups] + [down(c, j) for j in range(D_MODEL // MXU_COLS)]
    return units


def _norm_units(rs, src, sink):
    step = min(NORM_ROWS, rs.stop - rs.start)

    def unit(r0):
        def run():
            rows = slice(r0, r0 + step)
            sink(rows, src(rows))
        return run
    return [unit(r0) for r0 in range(rs.start, rs.stop, step)]


def _pipeline_units(parts, phases):
    units = []
    for s in range(parts + len(phases) - 1):
        live = [phase(s - p) for p, phase in enumerate(phases) if 0 <= s - p < parts]
        units += _interleave(*live)
    return units


def _run_pipeline(parts, phases):
    for unit in _pipeline_units(parts, phases):
        unit()


def _rope(x, cos, sin):
    lane = lax.broadcasted_iota(jnp.int32, (x.shape[0], LANES), 1) & (DIFF_HD - 1)
    first = lane < ROPE_HALF
    second = (lane >= ROPE_HALF) & (lane < ROPE_DIMS)
    parts = []
    for j in range(x.shape[1] // LANES):
        xj = x[:, j * LANES:(j + 1) * LANES]
        ahead = pltpu.roll(xj, LANES - ROPE_HALF, axis=1)
        behind = pltpu.roll(xj, ROPE_HALF, axis=1)
        parts.append(jnp.where(first, xj * cos - ahead * sin,
                               jnp.where(second, xj * cos + behind * sin, xj)))
    return jnp.concatenate(parts, axis=-1)


def _rope_feature_major(xt, cos, sin):
    pieces = []
    for m in range(xt.shape[0] // DIFF_HD):
        base = m * DIFF_HD
        a = xt[base:base + ROPE_HALF]
        b = xt[base + ROPE_HALF:base + ROPE_DIMS]
        pieces += [a * cos - b * sin, b * cos + a * sin, xt[base + ROPE_DIMS:base + DIFF_HD]]
    return jnp.concatenate(pieces, axis=0)


def _mlp_kv_kernel(x1_ref, wup_ref, wdn_ref, lng_ref, lnb_ref, wk_ref, wv_ref, wq_ref, cos_ref, sin_ref,
                   kcos_ref, ksin_ref, x2_ref, k_ref, v_ref, q_ref, *rest, feature_major_k, parts):
    if feature_major_k:
        kb_ref, vb_ref, x2b_scr = rest
    else:
        (x2b_scr,) = rest
    part = x1_ref.shape[0] // parts
    states = {}
    col_starts = range(0, D_MODEL, MXU_COLS)

    def mlp_phase(i):
        rs = slice(i * part, (i + 1) * part)
        st = states[i] = {}

        def start():
            st["xb"] = x1_ref[rs, :].astype(BF16)

        return [start] + _mlp_units(st, wup_ref, wdn_ref)

    def norm_phase(i):
        rs = slice(i * part, (i + 1) * part)
        st = states.pop(i)

        def gather():
            st["y"] = jnp.concatenate([st["y"][j] for j in range(len(col_starts))], axis=-1)

        def src(rows):
            local = slice(rows.start - rs.start, rows.stop - rs.start)
            return _layer_norm(DEEPNORM_ALPHA * x1_ref[rows, :] + st["y"][local], lng_ref[...], lnb_ref[...])

        def sink(rows, x2):
            x2_ref[rows, :] = x2
            x2b_scr[rows, :] = x2.astype(BF16)

        return [gather] + _norm_units(rs, src, sink)

    def proj_phase(i):
        rs = slice(i * part, (i + 1) * part)
        st = {}

        def start():
            st["xb"] = x2b_scr[rs, :]

        def proj_k(c0):
            def run():
                cs = slice(c0, c0 + MXU_COLS)
                if feature_major_k:
                    kt = _rope_feature_major(_dot_nt(wk_ref[cs, :], st["xb"]), kcos_ref[:, rs], ksin_ref[:, rs])
                    k_ref[0, cs, rs] = kt
                    kb_ref[0, cs, rs] = kt.astype(BF16)
                else:
                    k_ref[rs, cs] = _rope(_dot(st["xb"], wk_ref[:, cs]), kcos_ref[rs, :], ksin_ref[rs, :])
            return run

        def proj_v(c0):
            def run():
                v = _dot(st["xb"], wv_ref[:, c0:c0 + MXU_COLS])
                v_ref[rs, c0:c0 + MXU_COLS] = v
                if feature_major_k:
                    vb_ref[rs, c0:c0 + MXU_COLS] = v.astype(BF16)
            return run

        def proj_q(c0):
            def run():
                q = _rope(_dot(st["xb"], wq_ref[:, c0:c0 + MXU_COLS]), cos_ref[rs, :], sin_ref[rs, :])
                q_ref[rs, c0:c0 + MXU_COLS] = (q * (DIFF_HD ** -0.5 * LOG2E)).astype(BF16)
            return run

        return [start] + [f(c0) for c0 in col_starts for f in (proj_k, proj_v, proj_q)]

    _run_pipeline(parts, (mlp_phase, norm_phase, proj_phase))


def _mlp_kv(x1, wup, wdn, lng, lnb, wk, wv, wq, tables, *, batch, seq, feature_major_k):
    n = x1.shape[0]
    tm = min(TOKEN_TILE, n)
    cos, sin, cos_t, sin_t = tables
    if tm > seq:
        cos = jnp.tile(cos, (tm // seq, 1))
        sin = jnp.tile(sin, (tm // seq, 1))
    npos = max(seq // tm, 1)
    row = lambda w: pl.BlockSpec((tm, w), lambda i: (i, 0))
    pos = pl.BlockSpec((tm, LANES), lambda i: (i % npos, 0))
    if feature_major_k:
        kcos, ksin = cos_t, sin_t
        kpos = pl.BlockSpec((ROPE_HALF, tm), lambda i: (0, i % npos))
        k_spec = pl.BlockSpec((1, D_MODEL, tm), lambda i: (i // npos, 0, i % npos))
        k_shape = jax.ShapeDtypeStruct((batch, D_MODEL, seq), F32)
    else:
        kcos, ksin, kpos = cos, sin, pos
        k_spec = row(D_MODEL)
        k_shape = jax.ShapeDtypeStruct((n, D_MODEL), F32)
    out_specs = [row(D_MODEL), k_spec, row(D_MODEL), row(D_MODEL)]
    out_shape = [jax.ShapeDtypeStruct((n, D_MODEL), F32), k_shape,
                 jax.ShapeDtypeStruct((n, D_MODEL), F32), jax.ShapeDtypeStruct((n, D_MODEL), BF16)]
    if feature_major_k:
        out_specs += [k_spec, row(D_MODEL)]
        out_shape += [jax.ShapeDtypeStruct(k_shape.shape, BF16), jax.ShapeDtypeStruct((n, D_MODEL), BF16)]
    return pl.pallas_call(
        functools.partial(_mlp_kv_kernel, feature_major_k=feature_major_k, parts=TOKEN_PARTS),
        grid=(n // tm,),
        in_specs=[row(D_MODEL), _resident(wup.shape), _resident(wdn.shape), _resident(lng.shape),
                  _resident(lnb.shape), _resident(wk.shape), _resident(wv.shape), _resident(wq.shape),
                  pos, pos, kpos, kpos],
        out_specs=out_specs,
        out_shape=out_shape,
        scratch_shapes=[pltpu.VMEM((tm, D_MODEL), BF16)],
        compiler_params=_params(("parallel",)),
        name="mlp_kv",
    )(x1, wup, wdn, lng, lnb, wk, wv, wq, cos, sin, kcos, ksin)


def _out_mlp_kernel(o_ref, x2_ref, wo_ref, lng1_ref, lnb1_ref, wup_ref, wdn_ref, lng2_ref, lnb2_ref, y_ref,
                    x3_scr, x3b_scr, *, parts):
    part = o_ref.shape[0] // parts
    states = {}
    col_starts = range(0, D_MODEL, MXU_COLS)

    def mix_phase(i):
        rs = slice(i * part, (i + 1) * part)
        st = {"mix": []}

        def project(c0):
            def run():
                st["mix"].append(_dot(o_ref[rs, :], wo_ref[:, c0:c0 + MXU_COLS]))
            return run

        def gather():
            st["mix"] = jnp.concatenate(st["mix"], axis=-1)

        def src(rows):
            local = slice(rows.start - rs.start, rows.stop - rs.start)
            return _layer_norm(DEEPNORM_ALPHA * x2_ref[rows, :] + st["mix"][local], lng1_ref[...], lnb1_ref[...])

        def sink(rows, x3):
            x3_scr[rows, :] = x3
            x3b_scr[rows, :] = x3.astype(BF16)

        return [project(c0) for c0 in col_starts] + [gather] + _norm_units(rs, src, sink)

    def mlp_phase(i):
        rs = slice(i * part, (i + 1) * part)
        st = states[i] = {}

        def start():
            st["xb"] = x3b_scr[rs, :]

        return [start] + _mlp_units(st, wup_ref, wdn_ref)

    def final_phase(i):
        rs = slice(i * part, (i + 1) * part)
        st = states.pop(i)

        def gather():
            st["y"] = jnp.concatenate([st["y"][j] for j in range(len(col_starts))], axis=-1)

        def src(rows):
            local = slice(rows.start - rs.start, rows.stop - rs.start)
            return _layer_norm(DEEPNORM_ALPHA * x3_scr[rows, :] + st["y"][local], lng2_ref[...], lnb2_ref[...])

        def sink(rows, y):
            y_ref[rows, :] = y

        return [gather] + _norm_units(rs, src, sink)

    _run_pipeline(parts, (mix_phase, mlp_phase, final_phase))


def _out_mlp(o, x2, wo, lng1, lnb1, wup, wdn, lng2, lnb2):
    n = x2.shape[0]
    tm = min(TOKEN_TILE, n)
    row = lambda w: pl.BlockSpec((tm, w), lambda i: (i, 0))
    return pl.pallas_call(
        functools.partial(_out_mlp_kernel, parts=TOKEN_PARTS),
        grid=(n // tm,),
        in_specs=[row(D_MODEL), row(D_MODEL), _resident(wo.shape), _resident(lng1.shape),
                  _resident(lnb1.shape), _resident(wup.shape), _resident(wdn.shape),
                  _resident(lng2.shape), _resident(lnb2.shape)],
        out_specs=row(D_MODEL),
        out_shape=jax.ShapeDtypeStruct((n, D_MODEL), F32),
        scratch_shapes=[pltpu.VMEM((tm, D_MODEL), F32), pltpu.VMEM((tm, D_MODEL), BF16)],
        compiler_params=_params(("parallel",)),
        name="out_mlp",
    )(o, x2, wo, lng1, lnb1, wup, wdn, lng2, lnb2)


def _lambda_full(lam):
    t1 = jnp.sum(lam[0:1, :] * lam[1:2, :], axis=-1, keepdims=True)
    t2 = jnp.sum(lam[2:3, :] * lam[3:4, :], axis=-1, keepdims=True)
    return jnp.exp(t1) - jnp.exp(t2) + LAMBDA_INIT


def _stack_maps(q):
    qf = q.astype(F32)
    lane = lax.broadcasted_iota(jnp.int32, qf.shape, 1)
    return jnp.concatenate([jnp.where(lane < DIFF_HD, qf, 0.0), jnp.where(lane >= DIFF_HD, qf, 0.0)],
                           axis=0).astype(BF16)


def _head_norm(acc, gn):
    return (acc * _rms_scale(acc) * gn * (1.0 - LAMBDA_INIT)).astype(BF16)


def _lane_fold(x, op):
    out = x[:, 0:LANES]
    for j in range(1, x.shape[1] // LANES):
        out = op(out, x[:, j * LANES:(j + 1) * LANES])
    return out


def _attention_units(q_ref, kt_ref, v_ref, s_scr, lam, gn, store, *, tq, order):
    chunk_shift = CHUNK.bit_length() - 1
    rc = (lax.broadcasted_iota(jnp.int32, (2 * tq, tq), 0) & (tq - 1)) >> chunk_shift
    cc = lax.broadcasted_iota(jnp.int32, (2 * tq, tq), 1) >> chunk_shift
    visible = cc <= rc

    states = {}
    slots = s_scr.shape[0]

    def score_phase(pos):
        i = order[pos]
        st = states[pos] = {}
        slot = pos % slots

        def start():
            st["qq"] = _stack_maps(q_ref[0, i * tq:(i + 1) * tq, :])

        def block(j):
            def run():
                s = _dot(st["qq"], kt_ref[0, :, j * tq:(j + 1) * tq])
                if j == i:
                    s = jnp.where(visible, s, -jnp.inf)
                s_scr[slot, j] = s
                f = _lane_fold(s, jnp.maximum)
                st["m"] = f if j == 0 else jnp.maximum(st["m"], f)
            return run

        def finish():
            st["m"] = jnp.max(st["m"], axis=-1, keepdims=True)

        return [start] + [block(j) for j in range(i + 1)] + [finish]

    def exp_phase(pos):
        i = order[pos]
        st = states[pos]
        slot = pos % slots

        def exp_block(j):
            def run():
                e = jnp.exp2(s_scr[slot, j] - st["m"])
                s_scr[slot, j] = e
                f = _lane_fold(e, jnp.add)
                st["l"] = f if j == 0 else st["l"] + f
            return run

        def weights():
            l = jnp.sum(st["l"], axis=-1, keepdims=True)
            st["inv_l0"] = 1.0 / l[0:tq]
            st["rho"] = lam * l[0:tq] / l[tq:]

        return [exp_block(j) for j in range(i + 1)] + [weights]

    def value_phase(pos):
        i = order[pos]
        st = states.pop(pos)
        slot = pos % slots

        def pv_block(j):
            def run():
                e = s_scr[slot, j]
                a = (e[0:tq] - e[tq:] * st["rho"]).astype(BF16)
                p = _dot(a, v_ref[0, j * tq:(j + 1) * tq, :])
                st["acc"] = p if j == 0 else st["acc"] + p
            return run

        def finish():
            store(i, _head_norm(st["acc"] * st["inv_l0"], gn))

        return [pv_block(j) for j in range(i + 1)] + [finish]

    return _pipeline_units(len(order), (score_phase, exp_phase, value_phase))


def _attn_mlp_kernel(lam_ref, gn_ref, q_ref, kt_ref, v_ref, x2_ref, wo_ref, lng1_ref, lnb1_ref, wup_ref, wdn_ref,
                     lng2_ref, lnb2_ref, y_ref, s_scr, o_scr, x3_scr, x3b_scr, *, tq, nq, batch):
    b = pl.program_id(0)
    h = pl.program_id(1)
    cur = b % 2
    part = x2_ref.shape[1]

    def attention(order):
        def store(i, o):
            o_scr[cur, h, i * tq:(i + 1) * tq, :] = o

        return _attention_units(q_ref, kt_ref, v_ref, s_scr, _lambda_full(lam_ref[...]), gn_ref[...], store,
                                tq=tq, order=order)

    growing = list(range(nq))
    shrinking = growing[::-1]

    def channel_mixer():
        r0 = pl.multiple_of(h * part, part)
        st = {"mix": []}
        col_starts = range(0, D_MODEL, MXU_COLS)

        def gather_heads():
            st["o"] = jnp.concatenate([o_scr[1 - cur, g, pl.ds(r0, part), :] for g in range(DIFF_HEADS)], axis=-1)

        def project(c0):
            def run():
                st["mix"].append(_dot(st["o"], wo_ref[:, c0:c0 + MXU_COLS]))
            return run

        def gather_mix():
            st["mix"] = jnp.concatenate(st["mix"], axis=-1)

        def norm1_src(rows):
            return _layer_norm(DEEPNORM_ALPHA * x2_ref[0, rows, :] + st["mix"][rows], lng1_ref[...], lnb1_ref[...])

        def norm1_sink(rows, x3):
            x3_scr[rows, :] = x3
            x3b_scr[rows, :] = x3.astype(BF16)

        def load():
            st["xb"] = x3b_scr[...]

        def gather_y():
            st["y"] = jnp.concatenate([st["y"][j] for j in range(len(col_starts))], axis=-1)

        def norm2_src(rows):
            return _layer_norm(DEEPNORM_ALPHA * x3_scr[rows, :] + st["y"][rows], lng2_ref[...], lnb2_ref[...])

        def norm2_sink(rows, y):
            y_ref[0, rows, :] = y

        whole = slice(0, part)
        return ([gather_heads] + [project(c0) for c0 in col_starts] + [gather_mix]
                + _norm_units(whole, norm1_src, norm1_sink) + [load] + _mlp_units(st, wup_ref, wdn_ref)
                + [gather_y] + _norm_units(whole, norm2_src, norm2_sink))

    @pl.when(b == 0)
    def _():
        for unit in attention(growing):
            unit()

    @pl.when((b > 0) & (b < batch))
    def _():
        for unit in _interleave(attention(shrinking), channel_mixer()):
            unit()

    @pl.when(b == batch)
    def _():
        for unit in channel_mixer():
            unit()


def _attn_mlp_prompt(q, kt, v, x2, lam, gn, wo, lng1, lnb1, wup, wdn, lng2, lnb2):
    batch, seq, _ = q.shape
    tq = min(ATT_TILE, seq)
    nq = seq // tq
    part = seq // DIFF_HEADS
    last = batch - 1
    rows = pl.BlockSpec((1, seq, HEAD_W), lambda b, h: (jnp.minimum(b, last), 0, h))
    cols = pl.BlockSpec((1, HEAD_W, seq), lambda b, h: (jnp.minimum(b, last), h, 0))
    lagged = pl.BlockSpec((1, part, D_MODEL),
                          lambda b, h: (jnp.maximum(b - 1, 0), jnp.where(b == 0, 0, h), 0))
    weights = [wo, lng1, lnb1, wup, wdn, lng2, lnb2]
    return pl.pallas_call(
        functools.partial(_attn_mlp_kernel, tq=tq, nq=nq, batch=batch),
        grid=(batch + 1, DIFF_HEADS),
        in_specs=[_resident(lam.shape), _resident(gn.shape), rows, cols, rows, lagged]
                 + [_resident(w.shape) for w in weights],
        out_specs=lagged,
        out_shape=jax.ShapeDtypeStruct((batch, seq, D_MODEL), F32),
        scratch_shapes=[pltpu.VMEM((ATT_TILES_IN_FLIGHT, nq, 2 * tq, tq), F32),
                        pltpu.VMEM((2, DIFF_HEADS, seq, HEAD_W), BF16),
                        pltpu.VMEM((part, D_MODEL), F32), pltpu.VMEM((part, D_MODEL), BF16)],
        compiler_params=_params(("arbitrary", "arbitrary")),
        name="attn_mlp",
    )(lam, gn, q, kt, v, x2, *weights)


def _attn_sample_kernel(lam_ref, gn_ref, q_ref, ckt_ref, cv_ref, kn_ref, vn_ref, o_ref):
    lam = _lambda_full(lam_ref[...])
    gn = gn_ref[...]
    t = q_ref.shape[1]
    past = ckt_ref.shape[2]
    live = {}

    def stage_scores(h):
        hs = slice(h * HEAD_W, (h + 1) * HEAD_W)
        qq = _stack_maps(q_ref[0, :, hs])
        live[h] = dict(hs=hs, sp=_dot(qq, ckt_ref[0, hs, :].astype(BF16)),
                       sn=_dot_nt(qq, kn_ref[0, :, hs].astype(BF16)))

    def stage_softmax(h):
        it = live[h]
        sp, sn = it.pop("sp"), it.pop("sn")
        m = jnp.maximum(jnp.max(sp, axis=-1, keepdims=True), jnp.max(sn, axis=-1, keepdims=True))
        ep = jnp.exp2(sp - m)
        en = jnp.exp2(sn - m)
        l = jnp.sum(ep, axis=-1, keepdims=True) + jnp.sum(en, axis=-1, keepdims=True)
        rho = lam * l[0:t] / l[t:]
        it["ap"] = (ep[0:t] - ep[t:] * rho).astype(BF16)
        it["an"] = (en[0:t] - en[t:] * rho).astype(BF16)
        it["inv_l0"] = 1.0 / l[0:t]

    def stage_values(h):
        it = live.pop(h)
        cv = cv_ref[0, pl.ds(h, past, stride=DIFF_HEADS), :]
        acc = _dot(it["ap"], cv.astype(BF16)) + _dot(it["an"], vn_ref[0, :, it["hs"]].astype(BF16))
        o_ref[0, :, it["hs"]] = _head_norm(acc * it["inv_l0"], gn)

    stages = (stage_scores, stage_softmax, stage_values)
    for step in range(DIFF_HEADS + len(stages) - 1):
        for depth, stage in enumerate(stages):
            if 0 <= step - depth < DIFF_HEADS:
                stage(step - depth)


def _diff_attn_sample(q, k_new, v_new, cache_kt, cache_v, lam, gn):
    batch, t, _ = q.shape
    past = cache_v.shape[1]
    new = pl.BlockSpec((1, t, D_MODEL), lambda b: (b, 0, 0))
    return pl.pallas_call(
        _attn_sample_kernel,
        grid=(batch,),
        in_specs=[_resident(lam.shape), _resident(gn.shape), new,
                  pl.BlockSpec((1, D_MODEL, past), lambda b: (b, 0, 0)),
                  pl.BlockSpec((1, past * DIFF_HEADS, HEAD_W), lambda b: (b, 0, 0)), new, new],
        out_specs=new,
        out_shape=jax.ShapeDtypeStruct((batch, t, D_MODEL), BF16),
        compiler_params=_params(("parallel",)),
        name="diff_attn_sample",
    )(lam, gn, q, cache_kt, cache_v.reshape(batch, past * DIFF_HEADS, HEAD_W), k_new, v_new)


def _rope_tables(pos):
    inv = 1.0 / (ROPE_THETA ** (jnp.arange(0, ROPE_DIMS, 2, dtype=F32) / ROPE_DIMS))
    ang = pos.astype(F32)[:, None] * inv[None, :]
    cos, sin = jnp.cos(ang), jnp.sin(ang)
    idx = jnp.arange(LANES) % ROPE_HALF
    return cos[:, idx], sin[:, idx], cos.T, sin.T


def _prepare_weights(w_in_a, w_gate_up_a, b_gate_a, g_norm_a, w_o_a, w_kv, w_q_b, lam_b, g_norm_b, w_o_b,
                     w_up, w_down, ln_g, ln_b):
    n_main = 2 * GLA_DK + 2 * GLA_DV
    w_in = w_in_a[0]
    row = lambda a: a.reshape(1, -1)
    wk = w_kv[:, :D_MODEL].astype(BF16)
    return dict(
        wm=w_in[:, :n_main].astype(BF16),
        wgl=jnp.pad(w_in[:, n_main:], ((0, 0), (0, LANES - GATE_RANK))).astype(BF16),
        wg=jnp.pad(w_gate_up_a[0], ((0, LANES - GATE_RANK), (0, 0))).astype(BF16),
        bg=row(b_gate_a[0]),
        gn_a=row(g_norm_a[0]),
        wo_a=w_o_a[0].astype(BF16),
        wk=wk,
        wk_t=wk.T,
        wv=w_kv[:, D_MODEL:].astype(BF16),
        wq=w_q_b[0].astype(BF16),
        lam=lam_b[0],
        gn_b=row(g_norm_b[0]),
        wo_b=w_o_b[0].astype(BF16),
        wup=[w_up[i].astype(BF16) for i in range(DEPTH)],
        wdn=[w_down[i].astype(BF16) for i in range(DEPTH)],
        lng=[[row(ln_g[i, j]) for j in range(2)] for i in range(DEPTH)],
        lnb=[[row(ln_b[i, j]) for j in range(2)] for i in range(DEPTH)],
    )


def _trunk(x, pos, state_in, cache_k, cache_v, w):
    batch, seq, _ = x.shape
    n = batch * seq
    prompt = cache_k is None
    x2d = x.reshape(n, D_MODEL)
    x1, state = _gla_layer(x2d, state_in, w["wm"], w["wgl"], w["wg"], w["bg"], w["gn_a"], w["wo_a"],
                           w["lng"][0][0], w["lnb"][0][0], batch=batch, seq=seq, chunk=min(CHUNK, seq))
    x2, k_sh, v_sh, q_b, *kv_bf16 = _mlp_kv(x1, w["wup"][0], w["wdn"][0], w["lng"][0][1], w["lnb"][0][1],
                                            w["wk_t"] if prompt else w["wk"], w["wv"], w["wq"],
                                            _rope_tables(pos), batch=batch, seq=seq, feature_major_k=prompt)
    v3 = v_sh.reshape(batch, seq, D_MODEL)
    q3 = q_b.reshape(batch, seq, D_MODEL)
    layer_b = (w["wo_b"], w["lng"][1][0], w["lnb"][1][0], w["wup"][1], w["wdn"][1], w["lng"][1][1], w["lnb"][1][1])
    if prompt:
        kt_b, v_b = kv_bf16
        y = _attn_mlp_prompt(q3, kt_b, v_b.reshape(batch, seq, D_MODEL), x2.reshape(batch, seq, D_MODEL),
                             w["lam"], w["gn_b"], *layer_b)
        k_out = k_sh.reshape(batch, DIFF_HEADS, 2, DIFF_HD, seq).transpose(0, 4, 1, 2, 3)
    else:
        past = cache_k.shape[1]
        cache_kt = cache_k.transpose(0, 2, 3, 4, 1).reshape(batch, D_MODEL, past)
        o = _diff_attn_sample(q3, k_sh.reshape(batch, seq, D_MODEL), v3, cache_kt, cache_v, w["lam"], w["gn_b"])
        k_out = k_sh.reshape(batch, seq, DIFF_HEADS, 2, DIFF_HD)
        y = _out_mlp(o.reshape(n, D_MODEL), x2, *layer_b)
    return (y.reshape(batch, seq, D_MODEL), state[None], k_out, v3.reshape(batch, seq, DIFF_HEADS, HEAD_W))


def kernel(x_prompt, x_sample, state_gla, cache_k, cache_v, w_in_a, w_gate_up_a, b_gate_a, g_norm_a, w_o_a, w_kv, w_q_b, lam_b, g_norm_b, w_o_b, w_up, w_down, ln_g, ln_b):
    w = _prepare_weights(w_in_a, w_gate_up_a, b_gate_a, g_norm_a, w_o_a, w_kv, w_q_b, lam_b, g_norm_b,
                         w_o_b, w_up, w_down, ln_g, ln_b)
    batch, seq, _ = x_prompt.shape
    past = cache_k.shape[1]
    zero_state = jnp.zeros((batch, GLA_HEADS, GLA_DKH, GLA_DVH), F32)
    sample_out = _trunk(x_sample, past + jnp.arange(x_sample.shape[1]), state_gla[0], cache_k, cache_v, w)
    x_prompt, (y_s, s_s, k_s, v_s) = lax.optimization_barrier((x_prompt, sample_out))
    y_p, s_p, k_p, v_p = _trunk(x_prompt, jnp.arange(seq), zero_state, None, None, w)
    return (y_p, y_s, s_p, k_p, v_p, s_s, k_s, v_s)
```
